```python
import math
import jax, jax.numpy as jnp
from jax import lax
import numpy as np

D_MODEL = 1024
BATCH = 16
SEQ = 2048
DEPTH = 1
DEC_BATCH = 128
DEC_SEQ = 1
PAST_LEN = 8192
PAGE_SIZE = 128

DA_HEADS = 4
DA_QK = 64
DA_V = 2 * DA_QK
SB_HEADS = 8
SB_DIM = 64
X_HEADS = 4
X_DIM = 128
N_MEM = 256
D_FF = 4 * D_MODEL
N_BRANCH = 3
ROPE_THETA = 10000.0
Q_BLOCK = 128
EPS = 1e-6
DA_W = DA_HEADS * 2 * DA_QK
DA_VW = DA_HEADS * DA_V
SB_W = SB_HEADS * SB_DIM
X_W = X_HEADS * X_DIM
IN_COLS = 2 * DA_W + DA_VW + 3 * SB_W + X_W + N_BRANCH * D_MODEL

kernel_name = "gated_diff_stickbreak_memory_decoder_step"


def rmsnorm(x, g):
    xf = x.astype(jnp.float32)
    xf = xf * lax.rsqrt(jnp.mean(xf * xf, axis=-1, keepdims=True) + EPS)
    return (xf * g.astype(jnp.float32)).astype(x.dtype)


def rope(x, pos):
    d = x.shape[-1]
    freqs = ROPE_THETA ** (-jnp.arange(0, d, 2, dtype=jnp.float32) / d)
    ang = pos.astype(jnp.float32)[:, None] * freqs[None, :]
    cos = jnp.cos(ang)[:, None, None, :].astype(x.dtype)
    sin = jnp.sin(ang)[:, None, None, :].astype(x.dtype)
    x1, x2 = x[..., : d // 2], x[..., d // 2:]
    return jnp.concatenate([x1 * cos - x2 * sin, x1 * sin + x2 * cos], axis=-1)


def in_project(h, w_in):
    B, T, _ = h.shape
    sizes = [DA_W, DA_W, DA_VW, SB_W, SB_W, SB_W, X_W]
    idx = [int(v) for v in np.cumsum(sizes)]
    q_da, k_da, v_da, q_sb, k_sb, v_sb, q_x, g = jnp.split(h @ w_in, idx, axis=-1)
    return (q_da.reshape(B, T, DA_HEADS, 2, DA_QK), k_da.reshape(B, T, DA_HEADS, 2, DA_QK),
            v_da.reshape(B, T, DA_HEADS, DA_V), q_sb.reshape(B, T, SB_HEADS, SB_DIM),
            k_sb.reshape(B, T, SB_HEADS, SB_DIM), v_sb.reshape(B, T, SB_HEADS, SB_DIM),
            q_x.reshape(B, T, X_HEADS, X_DIM), g.reshape(B, T, N_BRANCH, D_MODEL))


def diff_core(q, k, v, q_pos, k_pos, lam):
    s = jnp.einsum('bqhmd,bkhmd->bhmqk', q, k).astype(jnp.float32) * (DA_QK ** -0.5)
    mask = k_pos[None, :] <= q_pos[:, None]
    p = jax.nn.softmax(jnp.where(mask, s, -jnp.inf), axis=-1)
    w = p[:, :, 0] - lam * p[:, :, 1]
    return jnp.einsum('bhqk,bkhe->bqhe', w.astype(v.dtype), v)


def sb_core(q, k, v, q_pos, k_pos):
    z = jnp.einsum('bqhd,bkhd->bhqk', q, k).astype(jnp.float32) * (SB_DIM ** -0.5)
    mask = k_pos[None, :] < q_pos[:, None]
    log_beta = jax.nn.log_sigmoid(z)
    log_1m = jnp.where(mask, jax.nn.log_sigmoid(-z), 0.0)
    rest = lax.cumsum(log_1m, axis=3, reverse=True) - log_1m
    a = jnp.where(mask, jnp.exp(log_beta + rest), 0.0)
    return jnp.einsum('bhqk,bkhd->bqhd', a.astype(v.dtype), v)


def cross_core(q, k, v):
    s = jnp.einsum('bqhd,bmhd->bhqm', q, k).astype(jnp.float32) * (X_DIM ** -0.5)
    p = jax.nn.softmax(s, axis=-1)
    return jnp.einsum('bhqm,bmhd->bqhd', p.astype(v.dtype), v)


def prompt_mixers(q_da, k_da, v_da, q_sb, k_sb, v_sb, pos, lam):
    B, S = q_da.shape[:2]
    nb = S // Q_BLOCK

    def blockify(a):
        return a.reshape((B, nb, Q_BLOCK) + a.shape[2:]).swapaxes(0, 1)

    def unblock(a):
        return a.swapaxes(0, 1).reshape((B, S) + a.shape[3:])

    def one(args):
        qd, qs, qp = args
        return diff_core(qd, k_da, v_da, qp, pos, lam), sb_core(qs, k_sb, v_sb, qp, pos)

    od, osb = lax.map(one, (blockify(q_da), blockify(q_sb), pos.reshape(nb, Q_BLOCK)))
    return unblock(od), unblock(osb)


def sample_mixers(q_da, k_da, v_da, q_sb, k_sb, v_sb, pos_new, page_table,
                  pool_k_da, pool_v_da, pool_k_sb, pool_v_sb, lam):
    past = page_table.shape[1] * PAGE_SIZE
    k_pos = jnp.arange(past + q_da.shape[1])

    def gather(pool, pages):
        return pool[pages].reshape((past,) + pool.shape[2:])

    def one(args):
        qd, kd, vd, qs, ks, vs, pages = args
        kd_all = jnp.concatenate([gather(pool_k_da, pages), kd], axis=0)[None]
        vd_all = jnp.concatenate([gather(pool_v_da, pages), vd], axis=0)[None]
        ks_all = jnp.concatenate([gather(pool_k_sb, pages), ks], axis=0)[None]
        vs_all = jnp.concatenate([gather(pool_v_sb, pages), vs], axis=0)[None]
        od = diff_core(qd[None], kd_all, vd_all, pos_new, k_pos, lam)[0]
        osb = sb_core(qs[None], ks_all, vs_all, pos_new, k_pos)[0]
        return od, osb

    return lax.map(one, (q_da, k_da, v_da, q_sb, k_sb, v_sb, page_table))


def merge_out(gate_logits, o_da, o_sb, o_x, w_br_da, w_br_sb, w_br_x, w_o):
    B, T = o_da.shape[:2]
    g = jax.nn.sigmoid(gate_logits)
    m = (g[:, :, 0] * (o_da.reshape(B, T, DA_VW) @ w_br_da)
         + g[:, :, 1] * (o_sb.reshape(B, T, SB_W) @ w_br_sb)
         + g[:, :, 2] * (o_x.reshape(B, T, X_W) @ w_br_x))
    return m @ w_o


def mlp(x, g, w_up, w_down):
    return jnp.square(jax.nn.relu(rmsnorm(x, g) @ w_up)) @ w_down


def setup_inputs(seed: int = 0) -> dict:
    key = jax.random.key(seed)
    ks = jax.random.split(key, 32)
    n_pages = PAST_LEN // PAGE_SIZE
    n_pool = (DEC_BATCH * n_pages * 5 + 3) // 4
    nrm = lambda k, shape, s=1.0: jax.random.normal(k, shape, jnp.float32) * s
    gain = lambda k, shape: 1.0 + 0.02 * jax.random.normal(k, shape, jnp.float32)
    perm = jax.random.permutation(ks[9], n_pool)[: DEC_BATCH * n_pages]
    return {
        "x_prompt": nrm(ks[0], (BATCH, SEQ, D_MODEL)),
        "x_sample": nrm(ks[1], (DEC_BATCH, DEC_SEQ, D_MODEL)),
        "mem_prompt": nrm(ks[2], (BATCH, N_MEM, D_MODEL)),
        "cache_k_da": nrm(ks[3], (DEPTH, n_pool, PAGE_SIZE, DA_HEADS, 2, DA_QK)),
        "cache_v_da": nrm(ks[4], (DEPTH, n_pool, PAGE_SIZE, DA_HEADS, DA_V)),
        "cache_k_sb": nrm(ks[5], (DEPTH, n_pool, PAGE_SIZE, SB_HEADS, SB_DIM)),
        "cache_v_sb": nrm(ks[6], (DEPTH, n_pool, PAGE_SIZE, SB_HEADS, SB_DIM)),
        "cache_mem_k": nrm(ks[7], (DEPTH, DEC_BATCH, N_MEM, X_HEADS, X_DIM)),
        "cache_mem_v": nrm(ks[8], (DEPTH, DEC_BATCH, N_MEM, X_HEADS, X_DIM)),
        "page_table": perm.reshape(DEC_BATCH, n_pages).astype(jnp.int32),
        "norm_attn": gain(ks[10], (DEPTH, D_MODEL)),
        "norm_mem": gain(ks[11], (DEPTH, D_MODEL)),
        "w_in": nrm(ks[12], (DEPTH, D_MODEL, IN_COLS), D_MODEL ** -0.5),
        "w_mem_kv": nrm(ks[13], (DEPTH, D_MODEL, 2 * X_W), D_MODEL ** -0.5),
        "lambda_q1": nrm(ks[14], (DEPTH, DA_QK), 0.1),
        "lambda_k1": nrm(ks[15], (DEPTH, DA_QK), 0.1),
        "lambda_q2": nrm(ks[16], (DEPTH, DA_QK), 0.1),
        "lambda_k2": nrm(ks[17], (DEPTH, DA_QK), 0.1),
        "subln_gain": gain(ks[18], (DEPTH, DA_V)),
        "w_br_da": nrm(ks[19], (DEPTH, DA_VW, D_MODEL), DA_VW ** -0.5),
        "w_br_sb": nrm(ks[20], (DEPTH, SB_W, D_MODEL), SB_W ** -0.5),
        "w_br_x": nrm(ks[21], (DEPTH, X_W, D_MODEL), X_W ** -0.5),
        "w_o": nrm(ks[22], (DEPTH, D_MODEL, D_MODEL), D_MODEL ** -0.5),
        "norm_mlp": gain(ks[23], (DEPTH, D_MODEL)),
        "w_up": nrm(ks[24], (DEPTH, D_MODEL, D_FF), D_MODEL ** -0.5),
        "w_down": nrm(ks[25], (DEPTH, D_FF, D_MODEL), D_FF ** -0.5),
        "norm_final": gain(ks[26], (D_MODEL,)),
    }


def reference(x_prompt, x_sample, mem_prompt, cache_k_da, cache_v_da, cache_k_sb, cache_v_sb,
              cache_mem_k, cache_mem_v, page_table, norm_attn, norm_mem, w_in, w_mem_kv,
              lambda_q1, lambda_k1, lambda_q2, lambda_k2, subln_gain, w_br_da, w_br_sb, w_br_x,
              w_o, norm_mlp, w_up, w_down, norm_final):
    S = x_prompt.shape[1]
    T = x_sample.shape[1]
    past = page_table.shape[1] * PAGE_SIZE
    pos_p = jnp.arange(S)
    pos_s = past + jnp.arange(T)
    xp, xs = x_prompt, x_sample
    pkd, pvd, pks, pvs, pmk, pmv = [], [], [], [], [], []
    skd, svd, sks, svs = [], [], [], []
    for l in range(DEPTH):
        lam_init = 0.8 - 0.6 * math.exp(-0.3 * l)
        lam = (jnp.exp(jnp.sum(lambda_q1[l].astype(jnp.float32) * lambda_k1[l].astype(jnp.float32)))
               - jnp.exp(jnp.sum(lambda_q2[l].astype(jnp.float32) * lambda_k2[l].astype(jnp.float32)))
               + lam_init)

        def post_da(o):
            return rmsnorm(o, subln_gain[l]) * (1.0 - lam_init)

        h = rmsnorm(xp, norm_attn[l])
        q_da, k_da, v_da, q_sb, k_sb, v_sb, q_x, g = in_project(h, w_in[l])
        q_da, k_da = rope(q_da, pos_p), rope(k_da, pos_p)
        mkv = rmsnorm(mem_prompt, norm_mem[l]) @ w_mem_kv[l]
        Bm = mem_prompt.shape[0]
        mem_k = mkv[..., :X_W].reshape(Bm, N_MEM, X_HEADS, X_DIM)
        mem_v = mkv[..., X_W:].reshape(Bm, N_MEM, X_HEADS, X_DIM)
        o_da, o_sb = prompt_mixers(q_da, k_da, v_da, q_sb, k_sb, v_sb, pos_p, lam)
        o_x = cross_core(q_x, mem_k, mem_v)
        xp = xp + merge_out(g, post_da(o_da), o_sb, o_x, w_br_da[l], w_br_sb[l], w_br_x[l], w_o[l])
        xp = xp + mlp(xp, norm_mlp[l], w_up[l], w_down[l])
        pkd.append(k_da); pvd.append(v_da); pks.append(k_sb); pvs.append(v_sb)
        pmk.append(mem_k); pmv.append(mem_v)

        h = rmsnorm(xs, norm_attn[l])
        q_da, k_da, v_da, q_sb, k_sb, v_sb, q_x, g = in_project(h, w_in[l])
        q_da, k_da = rope(q_da, pos_s), rope(k_da, pos_s)
        o_da, o_sb = sample_mixers(q_da, k_da, v_da, q_sb, k_sb, v_sb, pos_s, page_table,
                                   cache_k_da[l], cache_v_da[l], cache_k_sb[l], cache_v_sb[l], lam)
        o_x = cross_core(q_x, cache_mem_k[l], cache_mem_v[l])
        xs = xs + merge_out(g, post_da(o_da), o_sb, o_x, w_br_da[l], w_br_sb[l], w_br_x[l], w_o[l])
        xs = xs + mlp(xs, norm_mlp[l], w_up[l], w_down[l])
        skd.append(k_da); svd.append(v_da); sks.append(k_sb); svs.append(v_sb)

    y_prompt = rmsnorm(xp, norm_final)
    y_sample = rmsnorm(xs, norm_final)
    p_k_da, p_v_da = jnp.stack(pkd), jnp.stack(pvd)
    p_k_sb, p_v_sb = jnp.stack(pks), jnp.stack(pvs)
    p_mem_k, p_mem_v = jnp.stack(pmk), jnp.stack(pmv)
    s_k_da, s_v_da = jnp.stack(skd), jnp.stack(svd)
    s_k_sb, s_v_sb = jnp.stack(sks), jnp.stack(svs)
    return (y_prompt, y_sample, p_k_da, p_v_da, p_k_sb, p_v_sb, p_mem_k, p_mem_v,
            s_k_da, s_v_da, s_k_sb, s_v_sb)
```

```python
import functools
import math

import jax
import jax.numpy as jnp
from jax import lax
from jax.experimental import pallas as pl
from jax.experimental.pallas import tpu as pltpu

F32 = jnp.float32
BF16 = jnp.bfloat16

EPS = 1e-6
ROPE_THETA = 10000.0
DA_HEADS = 4
DA_QK = 64
DA_V = 2 * DA_QK
SB_HEADS = 8
SB_DIM = 64
X_HEADS = 4
X_DIM = 128
N_BRANCH = 3
BRANCH_W = 512
N_PROJ = 7
LANES = 128
STAT_ROWS = 16
NEG_BIG = -1e30
MIB = 1024 * 1024


def _cparams(semantics, vmem_mib):
    return pltpu.CompilerParams(dimension_semantics=semantics, vmem_limit_bytes=vmem_mib * MIB)


def _rms(x, g):
    return x * lax.rsqrt(jnp.mean(x * x, axis=-1, keepdims=True) + EPS) * g


def _dot(a, b):
    return jnp.dot(a, b, preferred_element_type=F32)


def _dot_nt(a, b):
    return lax.dot_general(a, b, (((1,), (1,)), ((), ())), preferred_element_type=F32)


def _neg_softplus(z):
    return -(jnp.maximum(z, 0.0) + jnp.log1p(jnp.exp(-jnp.abs(z))))


def _lam(lam_ref, lam_init):
    lv = lam_ref[...]
    s1 = jnp.sum(lv[0:1] * lv[1:2], axis=-1, keepdims=True)
    s2 = jnp.sum(lv[2:3] * lv[3:4], axis=-1, keepdims=True)
    return jnp.exp(s1) - jnp.exp(s2) + lam_init


def _inproj_kernel(x_ref, g_ref, w_ref, cos_ref, sin_ref,
                   qda_ref, kda_ref, kdab_ref, vda_ref, vdab_ref,
                   qsb_ref, ksb_ref, ksbb_ref, vsb_ref, vsbb_ref, qx_ref):
    w_cols = BRANCH_W
    h = _rms(x_ref[0], g_ref[...]).astype(BF16)

    def proj(c):
        return _dot(h, w_ref[:, c * w_cols:(c + 1) * w_cols])

    cos = cos_ref[...]
    sin = sin_ref[...]
    first = (lax.broadcasted_iota(jnp.int32, (1, w_cols), 1) % DA_QK) < (DA_QK // 2)

    def rope(t):
        partner = jnp.where(first, pltpu.roll(t, w_cols - DA_QK // 2, 1), pltpu.roll(t, DA_QK // 2, 1))
        return t * cos + partner * sin

    qda_ref[0] = (rope(proj(0)) * (DA_QK ** -0.5)).astype(BF16)
    kda = rope(proj(1))
    kda_ref[0] = kda
    kdab_ref[0] = kda.astype(BF16)
    vda = proj(2)
    vda_ref[0] = vda
    vdab_ref[0] = vda.astype(BF16)
    qsb_ref[0] = (proj(3) * (SB_DIM ** -0.5)).astype(BF16)
    ksb = proj(4)
    ksb_ref[0] = ksb
    ksbb_ref[0] = ksb.astype(BF16)
    vsb = proj(5)
    vsb_ref[0] = vsb
    vsbb_ref[0] = vsb.astype(BF16)
    qx_ref[0] = proj(6).astype(BF16)


def _inproj(x, g, w, cos, sin, tm):
    nb, s, d = x.shape
    grid = (s // tm, nb)
    wc = BRANCH_W
    row = lambda si, b: (b, si, 0)
    f32o = jax.ShapeDtypeStruct((nb, s, wc), F32)
    b16o = jax.ShapeDtypeStruct((nb, s, wc), BF16)
    ospec = pl.BlockSpec((1, tm, wc), row)
    return pl.pallas_call(
        _inproj_kernel,
        grid=grid,
        in_specs=[
            pl.BlockSpec((1, tm, d), row),
            pl.BlockSpec((1, d), lambda si, b: (0, 0)),
            pl.BlockSpec((d, N_PROJ * wc), lambda si, b: (0, 0)),
            pl.BlockSpec((tm, wc), lambda si, b: (si, 0)),
            pl.BlockSpec((tm, wc), lambda si, b: (si, 0)),
        ],
        out_specs=[ospec] * 11,
        out_shape=[b16o, f32o, b16o, f32o, b16o, b16o, f32o, b16o, f32o, b16o, b16o],
        compiler_params=_cparams(("arbitrary", "arbitrary"), 48),
        name="inproj",
    )(x, g, w, cos, sin)


def _memkv_kernel(x_ref, g_ref, w_ref, k_ref, kb_ref, v_ref, vb_ref):
    h = _rms(x_ref[0], g_ref[...]).astype(BF16)
    k = _dot(h, w_ref[:, :BRANCH_W])
    v = _dot(h, w_ref[:, BRANCH_W:])
    k_ref[0] = k
    kb_ref[0] = k.astype(BF16)
    v_ref[0] = v
    vb_ref[0] = v.astype(BF16)


def _memkv(mem, g, w):
    nb, nm, d = mem.shape
    wc = BRANCH_W
    row = lambda b: (b, 0, 0)
    f32o = jax.ShapeDtypeStruct((nb, nm, wc), F32)
    b16o = jax.ShapeDtypeStruct((nb, nm, wc), BF16)
    ospec = pl.BlockSpec((1, nm, wc), row)
    return pl.pallas_call(
        _memkv_kernel,
        grid=(nb,),
        in_specs=[
            pl.BlockSpec((1, nm, d), row),
            pl.BlockSpec((1, d), lambda b: (0, 0)),
            pl.BlockSpec((d, 2 * wc), lambda b: (0, 0)),
        ],
        out_specs=[ospec] * 4,
        out_shape=[f32o, b16o, f32o, b16o],
        compiler_params=_cparams(("arbitrary",), 32),
        name="memkv",
    )(mem, g, w)


def _da_kernel(lam_ref, gain_ref, q_ref, k_ref, v_ref, o_ref, acc_ref, m_ref, l_ref, *, tq, tk, lam_init):
    qi = pl.program_id(2)
    q = q_ref[0]
    lane = lax.broadcasted_iota(jnp.int32, (1, DA_V), 1)
    zero = jnp.zeros_like(q)
    q_maps = (jnp.where(lane < DA_QK, q, zero), jnp.where(lane >= DA_QK, q, zero))

    m_ref[...] = jnp.full(m_ref.shape, NEG_BIG, F32)
    l_ref[...] = jnp.zeros(l_ref.shape, F32)
    acc_ref[...] = jnp.zeros(acc_ref.shape, F32)

    def tile(j, masked):
        start = pl.multiple_of(j * tk, tk)
        k = k_ref[0, pl.ds(start, tk), :]
        v = v_ref[0, pl.ds(start, tk), :]
        if masked:
            keep = (lax.broadcasted_iota(jnp.int32, (tq, tk), 1)
                    <= lax.broadcasted_iota(jnp.int32, (tq, tk), 0))
        for mi in range(2):
            s = _dot_nt(q_maps[mi], k)
            if masked:
                s = jnp.where(keep, s, NEG_BIG)
            m_prev = m_ref[mi]
            m_new = jnp.maximum(m_prev, jnp.max(s, axis=-1, keepdims=True))
            alpha = jnp.exp(m_prev - m_new)
            p = jnp.exp(s - m_new)
            l_ref[mi] = alpha * l_ref[mi] + jnp.sum(p, axis=-1, keepdims=True)
            acc_ref[mi] = alpha * acc_ref[mi] + _dot(p.astype(BF16), v)
            m_ref[mi] = m_new

    def body(j, carry):
        tile(j, False)
        return carry

    lax.fori_loop(0, qi, body, 0)
    tile(qi, True)

    lam = _lam(lam_ref, lam_init)
    o = acc_ref[0] / l_ref[0] - lam * (acc_ref[1] / l_ref[1])
    o_ref[0] = (_rms(o, gain_ref[...]) * (1.0 - lam_init)).astype(o_ref.dtype)


def _da_attention(lam4, gain, q, k, v, lam_init, tq):
    nb, s, _ = q.shape
    tk = tq
    kern = functools.partial(_da_kernel, tq=tq, tk=tk, lam_init=lam_init)
    return pl.pallas_call(
        kern,
        grid=(nb, DA_HEADS, s // tq),
        in_specs=[
            pl.BlockSpec((4, DA_QK), lambda b, h, qi: (0, 0)),
            pl.BlockSpec((1, DA_V), lambda b, h, qi: (0, 0)),
            pl.BlockSpec((1, tq, DA_V), lambda b, h, qi: (b, qi, h)),
            pl.BlockSpec((1, s, DA_V), lambda b, h, qi: (b, 0, h)),
            pl.BlockSpec((1, s, DA_V), lambda b, h, qi: (b, 0, h)),
        ],
        out_specs=pl.BlockSpec((1, tq, DA_V), lambda b, h, qi: (b, qi, h)),
        out_shape=jax.ShapeDtypeStruct((nb, s, DA_HEADS * DA_V), BF16),
        scratch_shapes=[
            pltpu.VMEM((2, tq, DA_V), F32),
            pltpu.VMEM((2, tq, 1), F32),
            pltpu.VMEM((2, tq, 1), F32),
        ],
        compiler_params=_cparams(("arbitrary", "arbitrary", "arbitrary"), 32),
        name="da_attention",
    )(lam4, gain, q, k, v)


def _split_bf16(x):
    hi = x.astype(BF16)
    lo = (x - hi.astype(F32)).astype(BF16)
    return hi, lo


def _sb_kernel(q_ref, k_ref, v_ref, o_ref, acc_ref, c_ref, *, tq, tk):
    qi = pl.program_id(2)
    q = q_ref[0]
    lane = lax.broadcasted_iota(jnp.int32, (1, LANES), 1)
    zero = jnp.zeros_like(q)
    q_heads = (jnp.where(lane < SB_DIM, q, zero), jnp.where(lane >= SB_DIM, q, zero))
    later = (lax.broadcasted_iota(jnp.int32, (tk, tk), 0)
             > lax.broadcasted_iota(jnp.int32, (tk, tk), 1)).astype(BF16)

    c_ref[...] = jnp.zeros(c_ref.shape, F32)
    acc_ref[...] = jnp.zeros(acc_ref.shape, F32)

    def tile(j, masked):
        start = pl.multiple_of(j * tk, tk)
        k = k_ref[0, pl.ds(start, tk), :]
        v = v_ref[0, pl.ds(start, tk), :]
        if masked:
            keep = (lax.broadcasted_iota(jnp.int32, (tq, tk), 1)
                    < lax.broadcasted_iota(jnp.int32, (tq, tk), 0))
        for hi_ in range(2):
            z = _dot_nt(q_heads[hi_], k)
            log_1m = _neg_softplus(z)
            log_beta = z + log_1m
            if masked:
                log_1m = jnp.where(keep, log_1m, 0.0)
            hi, lo = _split_bf16(log_1m)
            rest = _dot(hi, later) + _dot(lo, later) + c_ref[hi_]
            a = jnp.exp(log_beta + rest)
            if masked:
                a = jnp.where(keep, a, 0.0)
            acc_ref[hi_] = acc_ref[hi_] + _dot(a.astype(BF16), v)
            c_ref[hi_] = c_ref[hi_] + jnp.sum(log_1m, axis=-1, keepdims=True)

    tile(qi, True)

    def body(jj, carry):
        tile(qi - 1 - jj, False)
        return carry

    lax.fori_loop(0, qi, body, 0)
    o_ref[0] = jnp.where(lane < SB_DIM, acc_ref[0], acc_ref[1]).astype(o_ref.dtype)


def _sb_attention(q, k, v, tq):
    nb, s, w = q.shape
    tk = tq
    n_pairs = w // LANES
    kern = functools.partial(_sb_kernel, tq=tq, tk=tk)
    return pl.pallas_call(
        kern,
        grid=(nb, n_pairs, s // tq),
        in_specs=[
            pl.BlockSpec((1, tq, LANES), lambda b, h, qi: (b, qi, h)),
            pl.BlockSpec((1, s, LANES), lambda b, h, qi: (b, 0, h)),
            pl.BlockSpec((1, s, LANES), lambda b, h, qi: (b, 0, h)),
        ],
        out_specs=pl.BlockSpec((1, tq, LANES), lambda b, h, qi: (b, qi, h)),
        out_shape=jax.ShapeDtypeStruct((nb, s, w), BF16),
        scratch_shapes=[
            pltpu.VMEM((2, tq, LANES), F32),
            pltpu.VMEM((2, tq, 1), F32),
        ],
        compiler_params=_cparams(("arbitrary", "arbitrary", "arbitrary"), 32),
        name="sb_attention",
    )(q, k, v)


def _cross_kernel(q_ref, k_ref, v_ref, o_ref):
    for h in range(X_HEADS):
        cols = slice(h * X_DIM, (h + 1) * X_DIM)
        s = _dot_nt(q_ref[0, :, cols], k_ref[0, :, cols]) * (X_DIM ** -0.5)
        p = jnp.exp(s - jnp.max(s, axis=-1, keepdims=True))
        p = p / jnp.sum(p, axis=-1, keepdims=True)
        o_ref[0, :, cols] = _dot(p.astype(BF16), v_ref[0, :, cols]).astype(o_ref.dtype)


def _cross_attention(q, mk, mv, tq):
    nb, s, w = q.shape
    nm = mk.shape[1]
    return pl.pallas_call(
        _cross_kernel,
        grid=(nb, s // tq),
        in_specs=[
            pl.BlockSpec((1, tq, w), lambda b, qi: (b, qi, 0)),
            pl.BlockSpec((1, nm, w), lambda b, qi: (b, 0, 0)),
            pl.BlockSpec((1, nm, w), lambda b, qi: (b, 0, 0)),
        ],
        out_specs=pl.BlockSpec((1, tq, w), lambda b, qi: (b, qi, 0)),
        out_shape=jax.ShapeDtypeStruct((nb, s, w), BF16),
        compiler_params=_cparams(("arbitrary", "arbitrary"), 32),
        name="cross_attention",
    )(q, mk, mv)


def _decode_kernel(pt_ref, lam_ref, gain_ref, expand_ref, later_ref, qd_ref, qs_ref, qx_ref, kdn_ref, vdn_ref,
                   mk_ref, mv_ref, *rest, pages_per_step, n_steps, lam_init):
    del pt_ref
    npg = pages_per_step
    kd_refs, vd_refs = rest[0:npg], rest[npg:2 * npg]
    ks_refs, vs_refs = rest[2 * npg:3 * npg], rest[3 * npg:4 * npg]
    oda_ref, osb_ref, ox_ref = rest[4 * npg:4 * npg + 3]
    qdm_ref, qsm_ref, accd_ref, accs_ref, m_ref, l_ref, c_ref = rest[4 * npg + 3:]
    j = pl.program_id(1)
    w = BRANCH_W
    row = lax.broadcasted_iota(jnp.int32, (STAT_ROWS, w), 0)
    col = lax.broadcasted_iota(jnp.int32, (STAT_ROWS, w), 1)
    row_v = lax.broadcasted_iota(jnp.int32, (STAT_ROWS, DA_V), 0)
    own_da = (lax.broadcasted_iota(jnp.int32, (npg * STAT_ROWS, w), 1) % DA_HEADS
              == (lax.broadcasted_iota(jnp.int32, (npg * STAT_ROWS, w), 0) % STAT_ROWS) // 2)

    def pick_rows(x, sel):
        return jnp.sum(jnp.where(sel, x, 0.0), axis=0, keepdims=True)

    @pl.when(j == 0)
    def _():
        qd = jnp.where(col // DA_QK == row, qd_ref[0], 0.0)
        qs = jnp.where(col // SB_DIM == row, qs_ref[0], 0.0)
        qdm_ref[...] = qd.astype(BF16)
        qsm_ref[...] = qs.astype(BF16)
        m_ref[...] = jnp.sum(qd * kdn_ref[0], axis=-1, keepdims=True)
        l_ref[...] = jnp.ones(l_ref.shape, F32)
        v_new = jnp.zeros((STAT_ROWS, DA_V), F32)
        for h in range(DA_HEADS):
            v_new = jnp.where(row_v // 2 == h, vdn_ref[0][:, h * DA_V:(h + 1) * DA_V], v_new)
        accd_ref[...] = v_new
        accs_ref[...] = jnp.zeros(accs_ref.shape, F32)
        c_ref[...] = jnp.zeros(c_ref.shape, F32)
        nmh = mk_ref.shape[1]
        qx = jnp.zeros((STAT_ROWS, X_DIM), F32)
        for h in range(X_HEADS):
            qx = jnp.where(row_v == h, qx_ref[0][:, h * X_DIM:(h + 1) * X_DIM], qx)
        own_x = (lax.broadcasted_iota(jnp.int32, (STAT_ROWS, nmh), 1) % X_HEADS
                 == lax.broadcasted_iota(jnp.int32, (STAT_ROWS, nmh), 0))
        s = _dot_nt(qx.astype(BF16), mk_ref[0].astype(BF16)) * (X_DIM ** -0.5)
        s = jnp.where(own_x, s, NEG_BIG)
        p = jnp.exp(s - jnp.max(s, axis=-1, keepdims=True))
        p = (p / jnp.sum(p, axis=-1, keepdims=True)).astype(BF16)
        ox = _dot(p, mv_ref[0].astype(BF16))
        for h in range(X_HEADS):
            ox_ref[0, :, h * X_DIM:(h + 1) * X_DIM] = ox[h:h + 1, :]

    page = kd_refs[0].shape[1]
    qd = qdm_ref[...]
    s = jnp.concatenate([_dot(qd, kd_refs[i][...].astype(BF16)) for i in range(npg)], axis=1)
    m_prev = m_ref[...]
    m_new = jnp.maximum(m_prev, jnp.max(s, axis=-1, keepdims=True))
    alpha = jnp.exp(m_prev - m_new)
    p = jnp.exp(s - m_new)
    l_ref[...] = alpha * l_ref[...] + jnp.sum(p, axis=-1, keepdims=True)
    m_ref[...] = m_new
    p = p.astype(BF16)
    p_rows = jnp.concatenate([p[:, i * page:(i + 1) * page] for i in range(npg)], axis=0)
    p2 = jnp.where(own_da, _dot(p_rows, expand_ref[...]), 0.0).astype(BF16)
    acc = alpha * accd_ref[...]
    for i in range(npg):
        acc = acc + _dot(p2[i * STAT_ROWS:(i + 1) * STAT_ROWS], vd_refs[i][...].astype(BF16))
    accd_ref[...] = acc

    qs = qsm_ref[...]
    z = jnp.concatenate([_dot(qs, ks_refs[i][...].astype(BF16)) for i in range(npg)], axis=1)
    log_1m = _neg_softplus(z)
    hi, lo = _split_bf16(log_1m)
    rest = _dot(jnp.concatenate([hi, lo], axis=0), later_ref[...])
    rest = rest[:STAT_ROWS] + rest[STAT_ROWS:] + c_ref[...]
    a = jnp.exp(z + log_1m + rest).astype(BF16)
    acc = accs_ref[...]
    for i in range(npg):
        acc = acc + _dot_nt(a[:, i * page:(i + 1) * page], vs_refs[i][...].astype(BF16))
    accs_ref[...] = acc
    c_ref[...] = c_ref[...] + jnp.sum(log_1m, axis=-1, keepdims=True)

    @pl.when(j == n_steps - 1)
    def _():
        lam = _lam(lam_ref, lam_init)
        accn = accd_ref[...] / l_ref[...]
        for h in range(DA_HEADS):
            o = accn[2 * h:2 * h + 1, :] - lam * accn[2 * h + 1:2 * h + 2, :]
            oda_ref[0, :, h * DA_V:(h + 1) * DA_V] = _rms(o, gain_ref[...]) * (1.0 - lam_init)
        osb_ref[0] = pick_rows(accs_ref[...], col // SB_DIM == row)


def _decode_attention(page_table, lam4, gain, qd, qs, qx, kdn, vdn, mem_k, mem_v,
                      pool_kd, pool_vd, pool_ks, pool_vs, lam_init, pages_per_step):
    nseq, n_pages = page_table.shape
    w = BRANCH_W
    npg = pages_per_step
    n_steps = n_pages // npg
    page = pool_kd.shape[2]
    pt_flat = page_table.reshape(-1)
    expand = (jnp.arange(page)[:, None] == jnp.arange(page * DA_HEADS)[None, :] // DA_HEADS).astype(BF16)
    step_tokens = jnp.arange(npg * page)
    later = (step_tokens[:, None] > step_tokens[None, :]).astype(BF16)

    seq = lambda b, j, pt: (b, 0, 0)
    const = lambda b, j, pt: (0, 0)

    def page_spec(pool, i):
        return pl.BlockSpec((None,) + pool.shape[1:],
                            lambda b, j, pt, i=i: (pt[b * n_pages + n_pages - (j + 1) * npg + i], 0, 0))

    row_spec = pl.BlockSpec((1, 1, w), seq)
    mem_spec = pl.BlockSpec((1,) + mem_k.shape[1:], seq)
    in_specs = ([pl.BlockSpec((4, DA_QK), const), pl.BlockSpec((1, DA_V), const),
                 pl.BlockSpec(expand.shape, const), pl.BlockSpec(later.shape, const)]
                + [row_spec] * 5 + [mem_spec] * 2
                + [page_spec(pool, i) for pool in (pool_kd, pool_vd, pool_ks, pool_vs) for i in range(npg)])
    out = jax.ShapeDtypeStruct((nseq, 1, w), F32)
    kern = functools.partial(_decode_kernel, pages_per_step=npg, n_steps=n_steps, lam_init=lam_init)
    return pl.pallas_call(
        kern,
        grid_spec=pltpu.PrefetchScalarGridSpec(
            num_scalar_prefetch=1,
            grid=(nseq, n_steps),
            in_specs=in_specs,
            out_specs=[row_spec] * 3,
            scratch_shapes=[
                pltpu.VMEM((STAT_ROWS, w), BF16),
                pltpu.VMEM((STAT_ROWS, w), BF16),
                pltpu.VMEM((STAT_ROWS, DA_V), F32),
                pltpu.VMEM((STAT_ROWS, w), F32),
                pltpu.VMEM((STAT_ROWS, 1), F32),
                pltpu.VMEM((STAT_ROWS, 1), F32),
                pltpu.VMEM((STAT_ROWS, 1), F32),
            ],
        ),
        out_shape=[out, out, out],
        compiler_params=_cparams(("arbitrary", "arbitrary"), 40),
        name="decode_attention",
    )(pt_flat, lam4, gain, expand, later, qd, qs, qx, kdn, vdn, mem_k, mem_v,
      *([pool_kd] * npg), *([pool_vd] * npg), *([pool_ks] * npg), *([pool_vs] * npg))


def _tail_kernel(x_ref, oda_ref, osb_ref, ox_ref, gattn_ref, wg_ref, wda_ref, wsb_ref, wx_ref, wo_ref,
                 gmlp_ref, wup_ref, wdn_ref, gfin_ref, y_ref, *, final_norm, ff_chunk):
    x = x_ref[...]
    d = x.shape[-1]
    h = _rms(x, gattn_ref[...]).astype(BF16)
    m = None
    for i, (o_ref, w_ref) in enumerate(((oda_ref, wda_ref), (osb_ref, wsb_ref), (ox_ref, wx_ref))):
        gate = jax.nn.sigmoid(_dot(h, wg_ref[:, i * d:(i + 1) * d]))
        term = gate * _dot(o_ref[...], w_ref[...])
        m = term if m is None else m + term
    x = x + _dot(m.astype(BF16), wo_ref[...])

    h = _rms(x, gmlp_ref[...]).astype(BF16)
    d_ff = wup_ref.shape[1]
    y = x
    for c in range(d_ff // ff_chunk):
        cols = slice(c * ff_chunk, (c + 1) * ff_chunk)
        u = jnp.square(jnp.maximum(_dot(h, wup_ref[:, cols]), 0.0))
        y = y + _dot(u.astype(BF16), wdn_ref[cols, :])
    if final_norm:
        y = _rms(y, gfin_ref[...])
    y_ref[...] = y


def _tail(x, oda, osb, ox, gattn, wg, wda, wsb, wx, wo, gmlp, wup, wdn, gfin, final_norm, tm):
    n, d = x.shape
    wc = BRANCH_W
    d_ff = wup.shape[1]
    row = lambda i: (i, 0)
    const = lambda i: (0, 0)

    def resident(shape):
        return pl.BlockSpec(shape, const, pipeline_mode=pl.Buffered(1))

    kern = functools.partial(_tail_kernel, final_norm=final_norm, ff_chunk=1024)
    return pl.pallas_call(
        kern,
        grid=(n // tm,),
        in_specs=[
            pl.BlockSpec((tm, d), row),
            pl.BlockSpec((tm, wc), row),
            pl.BlockSpec((tm, wc), row),
            pl.BlockSpec((tm, wc), row),
            resident((1, d)),
            resident((d, N_BRANCH * d)),
            resident((wc, d)),
            resident((wc, d)),
            resident((wc, d)),
            resident((d, d)),
            resident((1, d)),
            resident((d, d_ff)),
            resident((d_ff, d)),
            resident((1, d)),
        ],
        out_specs=pl.BlockSpec((tm, d), row),
        out_shape=jax.ShapeDtypeStruct((n, d), F32),
        compiler_params=_cparams(("arbitrary",), 56),
        name="tail",
    )(x, oda, osb, ox, gattn, wg, wda, wsb, wx, wo, gmlp, wup, wdn, gfin)


def _rope_tables(pos):
    half = DA_QK // 2
    freqs = ROPE_THETA ** (-jnp.arange(0, DA_QK, 2, dtype=F32) / DA_QK)
    ang = pos.astype(F32)[:, None] * freqs[None, :]
    cos, sin = jnp.cos(ang), jnp.sin(ang)
    reps = BRANCH_W // DA_QK
    cos_t = jnp.tile(jnp.concatenate([cos, cos], axis=-1), (1, reps))
    sin_t = jnp.tile(jnp.concatenate([-sin, sin], axis=-1), (1, reps))
    del half
    return cos_t, sin_t


def _feature_major(pool):
    perm = (0,) + tuple(range(2, pool.ndim)) + (1,)
    return jnp.transpose(pool, perm).reshape(pool.shape[0], BRANCH_W, pool.shape[1])


def _head_rows(a):
    return a.reshape(a.shape[0], a.shape[1] * a.shape[2], a.shape[3])


def _row_tile(n, want):
    return want if n % want == 0 else n


def kernel(x_prompt, x_sample, mem_prompt, cache_k_da, cache_v_da, cache_k_sb, cache_v_sb, cache_mem_k, cache_mem_v, page_table, norm_attn, norm_mem, w_in, w_mem_kv, lambda_q1, lambda_k1, lambda_q2, lambda_k2, subln_gain, w_br_da, w_br_sb, w_br_x, w_o, norm_mlp, w_up, w_down, norm_final):
    nb, s, d = x_prompt.shape
    nseq, t_new, _ = x_sample.shape
    assert t_new == 1, "the decode kernel handles one new token per sequence"
    depth = w_in.shape[0]
    n_pages = page_table.shape[1]
    page = cache_k_da.shape[2]
    n_pool = cache_k_da.shape[1]
    past = n_pages * page
    wc = BRANCH_W
    n_qkv = N_PROJ * wc

    cos_p, sin_p = _rope_tables(jnp.arange(s))
    cos_s, sin_s = _rope_tables(jnp.full((nseq,), past))
    gfin = norm_final.reshape(1, d)

    xp = x_prompt
    xs = x_sample.reshape(1, nseq, d)
    outs = {k: [] for k in ("pkd", "pvd", "pks", "pvs", "pmk", "pmv", "skd", "svd", "sks", "svs")}
    for l in range(depth):
        lam_init = 0.8 - 0.6 * math.exp(-0.3 * l)
        last = l == depth - 1
        w_qkv = w_in[l, :, :n_qkv].astype(BF16)
        w_gate = w_in[l, :, n_qkv:].astype(BF16)
        w_mem = w_mem_kv[l].astype(BF16)
        wda, wsb, wx = w_br_da[l].astype(BF16), w_br_sb[l].astype(BF16), w_br_x[l].astype(BF16)
        wo, wup, wdn = w_o[l].astype(BF16), w_up[l].astype(BF16), w_down[l].astype(BF16)
        gattn, gmem, gmlp = norm_attn[l].reshape(1, d), norm_mem[l].reshape(1, d), norm_mlp[l].reshape(1, d)
        lam4 = jnp.stack([lambda_q1[l], lambda_k1[l], lambda_q2[l], lambda_k2[l]]).astype(F32)
        gain = subln_gain[l].reshape(1, DA_V)
        tail_w = (gattn, w_gate, wda, wsb, wx, wo, gmlp, wup, wdn, gfin)

        (qda, kda, kdab, vda, vdab, qsb, ksb, ksbb, vsb, vsbb, qx) = _inproj(
            xp, gattn, w_qkv, cos_p, sin_p, _row_tile(s, 512))
        mem_k, mem_kb, mem_v, mem_vb = _memkv(mem_prompt, gmem, w_mem)
        tq = _row_tile(s, 256)
        o_da = _da_attention(lam4, gain, qda, kdab, vdab, lam_init, tq)
        o_sb = _sb_attention(qsb, ksbb, vsbb, tq)
        o_x = _cross_attention(qx, mem_kb, mem_vb, _row_tile(s, 512))
        n = nb * s
        xp = _tail(xp.reshape(n, d), o_da.reshape(n, wc), o_sb.reshape(n, wc), o_x.reshape(n, wc),
                   *tail_w, final_norm=last, tm=_row_tile(n, 256)).reshape(nb, s, d)
        outs["pkd"].append(kda.reshape(nb, s, DA_HEADS, 2, DA_QK))
        outs["pvd"].append(vda.reshape(nb, s, DA_HEADS, DA_V))
        outs["pks"].append(ksb.reshape(nb, s, SB_HEADS, SB_DIM))
        outs["pvs"].append(vsb.reshape(nb, s, SB_HEADS, SB_DIM))
        outs["pmk"].append(mem_k.reshape(nb, -1, X_HEADS, X_DIM))
        outs["pmv"].append(mem_v.reshape(nb, -1, X_HEADS, X_DIM))

        (qda, kda, kdab, vda, vdab, qsb, ksb, ksbb, vsb, vsbb, qx) = _inproj(
            xs, gattn, w_qkv, cos_s, sin_s, nseq)
        as_rows = lambda a: a.reshape(nseq, 1, wc).astype(F32)
        o_da, o_sb, o_x = _decode_attention(
            page_table, lam4, gain, as_rows(qda), as_rows(qsb), as_rows(qx), as_rows(kdab), as_rows(vdab),
            _head_rows(cache_mem_k[l]), _head_rows(cache_mem_v[l]),
            _feature_major(cache_k_da[l]), _head_rows(cache_v_da[l]),
            _feature_major(cache_k_sb[l]), _feature_major(cache_v_sb[l]),
            lam_init, pages_per_step=4 if n_pages % 4 == 0 else 1)
        as_b16 = lambda a: a.reshape(nseq, wc).astype(BF16)
        xs = _tail(xs.reshape(nseq, d), as_b16(o_da), as_b16(o_sb), as_b16(o_x),
                   *tail_w, final_norm=last, tm=nseq).reshape(1, nseq, d)
        outs["skd"].append(kda.reshape(nseq, 1, DA_HEADS, 2, DA_QK))
        outs["svd"].append(vda.reshape(nseq, 1, DA_HEADS, DA_V))
        outs["sks"].append(ksb.reshape(nseq, 1, SB_HEADS, SB_DIM))
        outs["svs"].append(vsb.reshape(nseq, 1, SB_HEADS, SB_DIM))

    st = lambda k: jnp.stack(outs[k])
    return (xp, xs.reshape(nseq, 1, d), st("pkd"), st("pvd"), st("pks"), st("pvs"), st("pmk"), st("pmv"),
            st("skd"), st("svd"), st("sks"), st("svs"))
```

```python
import functools
import math

import jax
import jax.numpy as jnp
from jax import lax
from jax.experimental import pallas as pl
from jax.experimental.pallas import tpu as pltpu

F32 = jnp.float32
BF16 = jnp.bfloat16

EPS = 1e-6
ROPE_THETA = 10000.0
DA_HEADS = 4
DA_QK = 64
DA_V = 2 * DA_QK
SB_HEADS = 8
SB_DIM = 64
X_HEADS = 4
X_DIM = 128
N_BRANCH = 3
BRANCH_W = 512
N_PROJ = 7
LANES = 128
STAT_ROWS = 16
DECODE_PAGES_PER_STEP = 8
SUFFIX_GROUP_PAGES = 4
SUFFIX_SUB_KEYS = 256
ATTN_TILE = 512
NEG_BIG = -1e30
MIB = 1024 * 1024


def _cparams(semantics, vmem_mib):
    return pltpu.CompilerParams(dimension_semantics=semantics, vmem_limit_bytes=vmem_mib * MIB)


def _rms(x, g):
    return x * lax.rsqrt(jnp.mean(x * x, axis=-1, keepdims=True) + EPS) * g


def _dot(a, b):
    return jnp.dot(a, b, preferred_element_type=F32)


def _dot_nt(a, b):
    return lax.dot_general(a, b, (((1,), (1,)), ((), ())), preferred_element_type=F32)


def _neg_softplus(z):
    return jnp.minimum(-z, 0.0) - jnp.log(1.0 + jnp.exp(-jnp.abs(z)))


def _lam(lam_ref, lam_init):
    lv = lam_ref[...]
    s1 = jnp.sum(lv[0:1] * lv[1:2], axis=-1, keepdims=True)
    s2 = jnp.sum(lv[2:3] * lv[3:4], axis=-1, keepdims=True)
    return jnp.exp(s1) - jnp.exp(s2) + lam_init


def _inproj_kernel(x_ref, g_ref, w_ref, cos_ref, sin_ref,
                   qda_ref, kda_ref, kdab_ref, vda_ref, vdab_ref,
                   qsb_ref, ksb_ref, ksbb_ref, vsb_ref, vsbb_ref, qx_ref):
    w_cols = BRANCH_W
    h = _rms(x_ref[0], g_ref[...]).astype(BF16)

    def proj(c):
        return _dot(h, w_ref[:, c * w_cols:(c + 1) * w_cols])

    cos = cos_ref[...]
    sin = sin_ref[...]
    first = (lax.broadcasted_iota(jnp.int32, (1, w_cols), 1) % DA_QK) < (DA_QK // 2)

    def rope(t):
        partner = jnp.where(first, pltpu.roll(t, w_cols - DA_QK // 2, 1), pltpu.roll(t, DA_QK // 2, 1))
        return t * cos + partner * sin

    qda_ref[0] = (rope(proj(0)) * (DA_QK ** -0.5)).astype(BF16)
    kda = rope(proj(1))
    kda_ref[0] = kda
    kdab_ref[0] = kda.astype(BF16)
    vda = proj(2)
    vda_ref[0] = vda
    vdab_ref[0] = vda.astype(BF16)
    qsb_ref[0] = (proj(3) * (SB_DIM ** -0.5)).astype(BF16)
    ksb = proj(4)
    ksb_ref[0] = ksb
    ksbb_ref[0] = ksb.astype(BF16)
    vsb = proj(5)
    vsb_ref[0] = vsb
    vsbb_ref[0] = vsb.astype(BF16)
    qx_ref[0] = proj(6).astype(BF16)


def _inproj(x, g, w, cos, sin, tm):
    nb, s, d = x.shape
    grid = (s // tm, nb)
    wc = BRANCH_W
    row = lambda si, b: (b, si, 0)
    f32o = jax.ShapeDtypeStruct((nb, s, wc), F32)
    b16o = jax.ShapeDtypeStruct((nb, s, wc), BF16)
    ospec = pl.BlockSpec((1, tm, wc), row)
    return pl.pallas_call(
        _inproj_kernel,
        grid=grid,
        in_specs=[
            pl.BlockSpec((1, tm, d), row),
            pl.BlockSpec((1, d), lambda si, b: (0, 0)),
            pl.BlockSpec((d, N_PROJ * wc), lambda si, b: (0, 0)),
            pl.BlockSpec((tm, wc), lambda si, b: (si, 0)),
            pl.BlockSpec((tm, wc), lambda si, b: (si, 0)),
        ],
        out_specs=[ospec] * 11,
        out_shape=[b16o, f32o, b16o, f32o, b16o, b16o, f32o, b16o, f32o, b16o, b16o],
        compiler_params=_cparams(("arbitrary", "arbitrary"), 48),
        name="inproj",
    )(x, g, w, cos, sin)


def _memkv_kernel(x_ref, g_ref, w_ref, k_ref, kb_ref, v_ref, vb_ref):
    h = _rms(x_ref[0], g_ref[...]).astype(BF16)
    k = _dot(h, w_ref[:, :BRANCH_W])
    v = _dot(h, w_ref[:, BRANCH_W:])
    k_ref[0] = k
    kb_ref[0] = k.astype(BF16)
    v_ref[0] = v
    vb_ref[0] = v.astype(BF16)


def _memkv(mem, g, w):
    nb, nm, d = mem.shape
    wc = BRANCH_W
    row = lambda b: (b, 0, 0)
    f32o = jax.ShapeDtypeStruct((nb, nm, wc), F32)
    b16o = jax.ShapeDtypeStruct((nb, nm, wc), BF16)
    ospec = pl.BlockSpec((1, nm, wc), row)
    return pl.pallas_call(
        _memkv_kernel,
        grid=(nb,),
        in_specs=[
            pl.BlockSpec((1, nm, d), row),
            pl.BlockSpec((1, d), lambda b: (0, 0)),
            pl.BlockSpec((d, 2 * wc), lambda b: (0, 0)),
        ],
        out_specs=[ospec] * 4,
        out_shape=[f32o, b16o, f32o, b16o],
        compiler_params=_cparams(("arbitrary",), 32),
        name="memkv",
    )(mem, g, w)


def _transpose_tiles(src_ref, dst_ref, tk):
    for c in range(dst_ref.shape[0]):
        dst_ref[c] = src_ref[0, c * tk:(c + 1) * tk, :].astype(F32).T.astype(dst_ref.dtype)


def _da_kernel(lam_ref, gain_ref, q_ref, k_ref, v_ref, o_ref, vt_ref, acc_ref, m_ref, l_ref, *, tq, tk, lam_init):
    qi = pl.program_id(2)

    @pl.when(qi == 0)
    def _():
        _transpose_tiles(v_ref, vt_ref, tk)

    q = q_ref[0]
    lane = lax.broadcasted_iota(jnp.int32, (1, DA_V), 1)
    zero = jnp.zeros_like(q)
    q2 = jnp.concatenate([jnp.where(lane < DA_QK, q, zero), jnp.where(lane >= DA_QK, q, zero)], axis=0)

    m_ref[...] = jnp.full(m_ref.shape, NEG_BIG, F32)
    l_ref[...] = jnp.zeros(l_ref.shape, F32)
    acc_ref[...] = jnp.zeros(acc_ref.shape, F32)

    def tile(j, masked):
        start = pl.multiple_of(j * tk, tk)
        s = _dot_nt(k_ref[0, pl.ds(start, tk), :], q2)
        if masked:
            key = lax.broadcasted_iota(jnp.int32, (tk, 2 * tq), 0)
            query = lax.broadcasted_iota(jnp.int32, (tk, 2 * tq), 1) % tq
            s = jnp.where(key <= query, s, NEG_BIG)
        m_prev = m_ref[...]
        m_new = jnp.maximum(m_prev, jnp.max(s, axis=0, keepdims=True))
        alpha = jnp.exp(m_prev - m_new)
        p = jnp.exp(s - m_new)
        l_ref[...] = alpha * l_ref[...] + jnp.sum(p, axis=0, keepdims=True)
        acc_ref[...] = alpha * acc_ref[...] + _dot(vt_ref[j], p.astype(BF16))
        m_ref[...] = m_new

    def body(j, carry):
        tile(j, False)
        return carry

    lax.fori_loop(0, qi, body, 0)
    tile(qi, True)

    lam = _lam(lam_ref, lam_init)
    accn = acc_ref[...] / l_ref[...]
    o = accn[:, :tq] - lam * accn[:, tq:]
    o = o * lax.rsqrt(jnp.mean(o * o, axis=0, keepdims=True) + EPS) * gain_ref[...] * (1.0 - lam_init)
    o_ref[0] = o.T.astype(o_ref.dtype)


def _da_attention(lam4, gain_col, q, k, v, lam_init, tq):
    nb, s, _ = q.shape
    tk = tq
    kern = functools.partial(_da_kernel, tq=tq, tk=tk, lam_init=lam_init)
    return pl.pallas_call(
        kern,
        grid=(nb, DA_HEADS, s // tq),
        in_specs=[
            pl.BlockSpec((4, DA_QK), lambda b, h, qi: (0, 0)),
            pl.BlockSpec((DA_V, 1), lambda b, h, qi: (0, 0)),
            pl.BlockSpec((1, tq, DA_V), lambda b, h, qi: (b, qi, h)),
            pl.BlockSpec((1, s, DA_V), lambda b, h, qi: (b, 0, h)),
            pl.BlockSpec((1, s, DA_V), lambda b, h, qi: (b, 0, h)),
        ],
        out_specs=pl.BlockSpec((1, tq, DA_V), lambda b, h, qi: (b, qi, h)),
        out_shape=jax.ShapeDtypeStruct((nb, s, DA_HEADS * DA_V), BF16),
        scratch_shapes=[
            pltpu.VMEM((s // tk, DA_V, tk), BF16),
            pltpu.VMEM((DA_V, 2 * tq), F32),
            pltpu.VMEM((1, 2 * tq), F32),
            pltpu.VMEM((1, 2 * tq), F32),
        ],
        compiler_params=_cparams(("arbitrary", "arbitrary", "arbitrary"), 32),
        name="da_attention",
    )(lam4, gain_col, q, k, v)


def _split_bf16(x):
    hi = x.astype(BF16)
    lo = (x - hi.astype(F32)).astype(BF16)
    return hi, lo


def _sb_kernel(q_ref, k_ref, v_ref, o_ref, vt_ref, acc_ref, c_ref, *, tq, tk):
    qi = pl.program_id(2)

    @pl.when(qi == 0)
    def _():
        _transpose_tiles(v_ref, vt_ref, tk)

    q = q_ref[0]
    lane = lax.broadcasted_iota(jnp.int32, (1, LANES), 1)
    zero = jnp.zeros_like(q)
    q2 = jnp.concatenate([jnp.where(lane < SB_DIM, q, zero), jnp.where(lane >= SB_DIM, q, zero)], axis=0)
    sub = math.gcd(tk, SUFFIX_SUB_KEYS)
    newer = jnp.where(lax.broadcasted_iota(jnp.int32, (sub, sub), 1) > lax.broadcasted_iota(jnp.int32, (sub, sub), 0),
                      1.0, 0.0).astype(BF16)

    c_ref[...] = jnp.zeros(c_ref.shape, F32)
    acc_ref[...] = jnp.zeros(acc_ref.shape, F32)

    def tile(j, masked):
        start = pl.multiple_of(j * tk, tk)
        z = _dot_nt(k_ref[0, pl.ds(start, tk), :], q2)
        log_1m = _neg_softplus(z)
        log_beta = z + log_1m
        if masked:
            key = lax.broadcasted_iota(jnp.int32, (tk, 2 * tq), 0)
            query = lax.broadcasted_iota(jnp.int32, (tk, 2 * tq), 1) % tq
            keep = key < query
            log_1m = jnp.where(keep, log_1m, 0.0)
        c = c_ref[...]
        rests = []
        for blk in reversed(range(tk // sub)):
            part = log_1m[blk * sub:(blk + 1) * sub]
            hi, lo = _split_bf16(part)
            rests.append(_dot(newer, hi) + _dot(newer, lo) + c)
            c = c + jnp.sum(part, axis=0, keepdims=True)
        c_ref[...] = c
        rest = jnp.concatenate(rests[::-1], axis=0)
        a = jnp.exp(log_beta + rest)
        if masked:
            a = jnp.where(keep, a, 0.0)
        a = a.astype(BF16)
        vt = vt_ref[j]
        acc_ref[:SB_DIM, :] = acc_ref[:SB_DIM, :] + _dot(vt[:SB_DIM], a[:, :tq])
        acc_ref[SB_DIM:, :] = acc_ref[SB_DIM:, :] + _dot(vt[SB_DIM:], a[:, tq:])

    tile(qi, True)

    def body(jj, carry):
        tile(qi - 1 - jj, False)
        return carry

    lax.fori_loop(0, qi, body, 0)
    o_ref[0] = acc_ref[...].T.astype(o_ref.dtype)


def _sb_attention(q, k, v, tq):
    nb, s, w = q.shape
    tk = tq
    n_pairs = w // LANES
    kern = functools.partial(_sb_kernel, tq=tq, tk=tk)
    return pl.pallas_call(
        kern,
        grid=(nb, n_pairs, s // tq),
        in_specs=[
            pl.BlockSpec((1, tq, LANES), lambda b, h, qi: (b, qi, h)),
            pl.BlockSpec((1, s, LANES), lambda b, h, qi: (b, 0, h)),
            pl.BlockSpec((1, s, LANES), lambda b, h, qi: (b, 0, h)),
        ],
        out_specs=pl.BlockSpec((1, tq, LANES), lambda b, h, qi: (b, qi, h)),
        out_shape=jax.ShapeDtypeStruct((nb, s, w), BF16),
        scratch_shapes=[
            pltpu.VMEM((s // tk, LANES, tk), BF16),
            pltpu.VMEM((LANES, tq), F32),
            pltpu.VMEM((1, 2 * tq), F32),
        ],
        compiler_params=_cparams(("arbitrary", "arbitrary", "arbitrary"), 32),
        name="sb_attention",
    )(q, k, v)


def _cross_kernel(q_ref, k_ref, v_ref, o_ref):
    for h in range(X_HEADS):
        cols = slice(h * X_DIM, (h + 1) * X_DIM)
        s = _dot_nt(q_ref[0, :, cols], k_ref[0, :, cols]) * (X_DIM ** -0.5)
        p = jnp.exp(s - jnp.max(s, axis=-1, keepdims=True))
        p = p / jnp.sum(p, axis=-1, keepdims=True)
        o_ref[0, :, cols] = _dot(p.astype(BF16), v_ref[0, :, cols]).astype(o_ref.dtype)


def _cross_attention(q, mk, mv, tq):
    nb, s, w = q.shape
    nm = mk.shape[1]
    return pl.pallas_call(
        _cross_kernel,
        grid=(nb, s // tq),
        in_specs=[
            pl.BlockSpec((1, tq, w), lambda b, qi: (b, qi, 0)),
            pl.BlockSpec((1, nm, w), lambda b, qi: (b, 0, 0)),
            pl.BlockSpec((1, nm, w), lambda b, qi: (b, 0, 0)),
        ],
        out_specs=pl.BlockSpec((1, tq, w), lambda b, qi: (b, qi, 0)),
        out_shape=jax.ShapeDtypeStruct((nb, s, w), BF16),
        compiler_params=_cparams(("arbitrary", "arbitrary"), 32),
        name="cross_attention",
    )(q, mk, mv)


def _decode_kernel(pt_ref, lam_ref, gain_ref, expand_ref, later_ref, qd_ref, qs_ref, qx_ref, kdn_ref, vdn_ref,
                   mk_ref, mv_ref, *rest, pages_per_step, n_steps, lam_init):
    del pt_ref
    npg = pages_per_step
    kd_refs, vd_refs = rest[0:npg], rest[npg:2 * npg]
    ks_refs, vs_refs = rest[2 * npg:3 * npg], rest[3 * npg:4 * npg]
    oda_ref, osb_ref, ox_ref = rest[4 * npg:4 * npg + 3]
    qdm_ref, qsm_ref, accd_ref, accs_ref, m_ref, l_ref, c_ref = rest[4 * npg + 3:]
    j = pl.program_id(1)
    w = BRANCH_W
    row = lax.broadcasted_iota(jnp.int32, (STAT_ROWS, w), 0)
    col = lax.broadcasted_iota(jnp.int32, (STAT_ROWS, w), 1)
    row_v = lax.broadcasted_iota(jnp.int32, (STAT_ROWS, DA_V), 0)
    own_da = (lax.broadcasted_iota(jnp.int32, (npg * STAT_ROWS, w), 1) % DA_HEADS
              == (lax.broadcasted_iota(jnp.int32, (npg * STAT_ROWS, w), 0) % STAT_ROWS) // 2)

    def pick_rows(x, sel):
        return jnp.sum(jnp.where(sel, x, 0.0), axis=0, keepdims=True)

    @pl.when(j == 0)
    def _():
        qd = jnp.where(col // DA_QK == row, qd_ref[0], 0.0)
        qs = jnp.where(col // SB_DIM == row, qs_ref[0], 0.0)
        qdm_ref[...] = qd.astype(BF16)
        qsm_ref[...] = qs.astype(BF16)
        m_ref[...] = jnp.sum(qd * kdn_ref[0], axis=-1, keepdims=True)
        l_ref[...] = jnp.ones(l_ref.shape, F32)
        v_new = jnp.zeros((STAT_ROWS, DA_V), F32)
        for h in range(DA_HEADS):
            v_new = jnp.where(row_v // 2 == h, vdn_ref[0][:, h * DA_V:(h + 1) * DA_V], v_new)
        accd_ref[...] = v_new
        accs_ref[...] = jnp.zeros(accs_ref.shape, F32)
        c_ref[...] = jnp.zeros(c_ref.shape, F32)
        nmh = mk_ref.shape[1]
        qx = jnp.zeros((STAT_ROWS, X_DIM), F32)
        for h in range(X_HEADS):
            qx = jnp.where(row_v == h, qx_ref[0][:, h * X_DIM:(h + 1) * X_DIM], qx)
        own_x = (lax.broadcasted_iota(jnp.int32, (STAT_ROWS, nmh), 1) % X_HEADS
                 == lax.broadcasted_iota(jnp.int32, (STAT_ROWS, nmh), 0))
        s = _dot_nt(qx.astype(BF16), mk_ref[0].astype(BF16)) * (X_DIM ** -0.5)
        s = jnp.where(own_x, s, NEG_BIG)
        p = jnp.exp(s - jnp.max(s, axis=-1, keepdims=True))
        p = (p / jnp.sum(p, axis=-1, keepdims=True)).astype(BF16)
        ox = _dot(p, mv_ref[0].astype(BF16))
        for h in range(X_HEADS):
            ox_ref[0, :, h * X_DIM:(h + 1) * X_DIM] = ox[h:h + 1, :]

    page = kd_refs[0].shape[1]
    qd = qdm_ref[...]
    s = jnp.concatenate([_dot(qd, kd_refs[i][...].astype(BF16)) for i in range(npg)], axis=1)
    m_prev = m_ref[...]
    m_new = jnp.maximum(m_prev, jnp.max(s, axis=-1, keepdims=True))
    alpha = jnp.exp(m_prev - m_new)
    p = jnp.exp(s - m_new)
    l_ref[...] = alpha * l_ref[...] + jnp.sum(p, axis=-1, keepdims=True)
    m_ref[...] = m_new
    p = p.astype(BF16)
    p_rows = jnp.concatenate([p[:, i * page:(i + 1) * page] for i in range(npg)], axis=0)
    p2 = jnp.where(own_da, _dot(p_rows, expand_ref[...]), 0.0).astype(BF16)
    acc = alpha * accd_ref[...]
    for i in range(npg):
        acc = acc + _dot(p2[i * STAT_ROWS:(i + 1) * STAT_ROWS], vd_refs[i][...].astype(BF16))
    accd_ref[...] = acc

    qs = qsm_ref[...]
    group = later_ref.shape[0] // page
    c = c_ref[...]
    acc = accs_ref[...]
    for g in reversed(range(npg // group)):
        pages = range(g * group, (g + 1) * group)
        z = jnp.concatenate([_dot(qs, ks_refs[i][...].astype(BF16)) for i in pages], axis=1)
        log_1m = _neg_softplus(z)
        hi, lo = _split_bf16(log_1m)
        rest = _dot(jnp.concatenate([hi, lo], axis=0), later_ref[...])
        rest = rest[:STAT_ROWS] + rest[STAT_ROWS:] + c
        a = jnp.exp(z + log_1m + rest).astype(BF16)
        for n, i in enumerate(pages):
            acc = acc + _dot_nt(a[:, n * page:(n + 1) * page], vs_refs[i][...].astype(BF16))
        c = c + jnp.sum(log_1m, axis=-1, keepdims=True)
    accs_ref[...] = acc
    c_ref[...] = c

    @pl.when(j == n_steps - 1)
    def _():
        lam = _lam(lam_ref, lam_init)
        accn = accd_ref[...] / l_ref[...]
        for h in range(DA_HEADS):
            o = accn[2 * h:2 * h + 1, :] - lam * accn[2 * h + 1:2 * h + 2, :]
            oda_ref[0, :, h * DA_V:(h + 1) * DA_V] = _rms(o, gain_ref[...]) * (1.0 - lam_init)
        osb_ref[0] = pick_rows(accs_ref[...], col // SB_DIM == row)


def _decode_attention(page_table, lam4, gain, qd, qs, qx, kdn, vdn, mem_k, mem_v,
                      pool_kd, pool_vd, pool_ks, pool_vs, lam_init, pages_per_step):
    nseq, n_pages = page_table.shape
    w = BRANCH_W
    npg = pages_per_step
    n_steps = n_pages // npg
    page = pool_kd.shape[2]
    pt_flat = page_table.reshape(-1)
    expand = (jnp.arange(page)[:, None] == jnp.arange(page * DA_HEADS)[None, :] // DA_HEADS).astype(BF16)
    group_tokens = jnp.arange(math.gcd(npg, SUFFIX_GROUP_PAGES) * page)
    later = (group_tokens[:, None] > group_tokens[None, :]).astype(BF16)

    seq = lambda b, j, pt: (b, 0, 0)
    const = lambda b, j, pt: (0, 0)

    def page_spec(pool, i):
        return pl.BlockSpec((None,) + pool.shape[1:],
                            lambda b, j, pt, i=i: (pt[b * n_pages + n_pages - (j + 1) * npg + i], 0, 0))

    row_spec = pl.BlockSpec((1, 1, w), seq)
    mem_spec = pl.BlockSpec((1,) + mem_k.shape[1:], seq)
    in_specs = ([pl.BlockSpec((4, DA_QK), const), pl.BlockSpec((1, DA_V), const),
                 pl.BlockSpec(expand.shape, const), pl.BlockSpec(later.shape, const)]
                + [row_spec] * 5 + [mem_spec] * 2
                + [page_spec(pool, i) for pool in (pool_kd, pool_vd, pool_ks, pool_vs) for i in range(npg)])
    out = jax.ShapeDtypeStruct((nseq, 1, w), F32)
    kern = functools.partial(_decode_kernel, pages_per_step=npg, n_steps=n_steps, lam_init=lam_init)
    return pl.pallas_call(
        kern,
        grid_spec=pltpu.PrefetchScalarGridSpec(
            num_scalar_prefetch=1,
            grid=(nseq, n_steps),
            in_specs=in_specs,
            out_specs=[row_spec] * 3,
            scratch_shapes=[
                pltpu.VMEM((STAT_ROWS, w), BF16),
                pltpu.VMEM((STAT_ROWS, w), BF16),
                pltpu.VMEM((STAT_ROWS, DA_V), F32),
                pltpu.VMEM((STAT_ROWS, w), F32),
                pltpu.VMEM((STAT_ROWS, 1), F32),
                pltpu.VMEM((STAT_ROWS, 1), F32),
                pltpu.VMEM((STAT_ROWS, 1), F32),
            ],
        ),
        out_shape=[out, out, out],
        compiler_params=_cparams(("arbitrary", "arbitrary"), 40),
        name="decode_attention",
    )(pt_flat, lam4, gain, expand, later, qd, qs, qx, kdn, vdn, mem_k, mem_v,
      *([pool_kd] * npg), *([pool_vd] * npg), *([pool_ks] * npg), *([pool_vs] * npg))


def _tail_kernel(x_ref, oda_ref, osb_ref, ox_ref, gattn_ref, wg_ref, wda_ref, wsb_ref, wx_ref, wo_ref,
                 gmlp_ref, wup_ref, wdn_ref, gfin_ref, y_ref, *, final_norm, ff_chunk):
    x = x_ref[...]
    d = x.shape[-1]
    h = _rms(x, gattn_ref[...]).astype(BF16)
    m = None
    for i, (o_ref, w_ref) in enumerate(((oda_ref, wda_ref), (osb_ref, wsb_ref), (ox_ref, wx_ref))):
        gate = jax.nn.sigmoid(_dot(h, wg_ref[:, i * d:(i + 1) * d]))
        term = gate * _dot(o_ref[...], w_ref[...])
        m = term if m is None else m + term
    x = x + _dot(m.astype(BF16), wo_ref[...])

    h = _rms(x, gmlp_ref[...]).astype(BF16)
    d_ff = wup_ref.shape[1]
    y = x
    for c in range(d_ff // ff_chunk):
        cols = slice(c * ff_chunk, (c + 1) * ff_chunk)
        u = jnp.square(jnp.maximum(_dot(h, wup_ref[:, cols]), 0.0))
        y = y + _dot(u.astype(BF16), wdn_ref[cols, :])
    if final_norm:
        y = _rms(y, gfin_ref[...])
    y_ref[...] = y


def _tail(x, oda, osb, ox, gattn, wg, wda, wsb, wx, wo, gmlp, wup, wdn, gfin, final_norm, tm):
    n, d = x.shape
    wc = BRANCH_W
    d_ff = wup.shape[1]
    row = lambda i: (i, 0)
    const = lambda i: (0, 0)

    def resident(shape):
        return pl.BlockSpec(shape, const, pipeline_mode=pl.Buffered(1))

    kern = functools.partial(_tail_kernel, final_norm=final_norm, ff_chunk=1024)
    return pl.pallas_call(
        kern,
        grid=(n // tm,),
        in_specs=[
            pl.BlockSpec((tm, d), row),
            pl.BlockSpec((tm, wc), row),
            pl.BlockSpec((tm, wc), row),
            pl.BlockSpec((tm, wc), row),
            resident((1, d)),
            resident((d, N_BRANCH * d)),
            resident((wc, d)),
            resident((wc, d)),
            resident((wc, d)),
            resident((d, d)),
            resident((1, d)),
            resident((d, d_ff)),
            resident((d_ff, d)),
            resident((1, d)),
        ],
        out_specs=pl.BlockSpec((tm, d), row),
        out_shape=jax.ShapeDtypeStruct((n, d), F32),
        compiler_params=_cparams(("arbitrary",), 56),
        name="tail",
    )(x, oda, osb, ox, gattn, wg, wda, wsb, wx, wo, gmlp, wup, wdn, gfin)


def _rope_tables(pos):
    half = DA_QK // 2
    freqs = ROPE_THETA ** (-jnp.arange(0, DA_QK, 2, dtype=F32) / DA_QK)
    ang = pos.astype(F32)[:, None] * freqs[None, :]
    cos, sin = jnp.cos(ang), jnp.sin(ang)
    reps = BRANCH_W // DA_QK
    cos_t = jnp.tile(jnp.concatenate([cos, cos], axis=-1), (1, reps))
    sin_t = jnp.tile(jnp.concatenate([-sin, sin], axis=-1), (1, reps))
    del half
    return cos_t, sin_t


def _feature_major(pool):
    perm = (0,) + tuple(range(2, pool.ndim)) + (1,)
    return jnp.transpose(pool, perm).reshape(pool.shape[0], BRANCH_W, pool.shape[1])


def _head_rows(a):
    return a.reshape(a.shape[0], a.shape[1] * a.shape[2], a.shape[3])


def _row_tile(n, want):
    return want if n % want == 0 else n


def kernel(x_prompt, x_sample, mem_prompt, cache_k_da, cache_v_da, cache_k_sb, cache_v_sb, cache_mem_k, cache_mem_v, page_table, norm_attn, norm_mem, w_in, w_mem_kv, lambda_q1, lambda_k1, lambda_q2, lambda_k2, subln_gain, w_br_da, w_br_sb, w_br_x, w_o, norm_mlp, w_up, w_down, norm_final):
    nb, s, d = x_prompt.shape
    nseq, t_new, _ = x_sample.shape
    assert t_new == 1, "the decode kernel handles one new token per sequence"
    depth = w_in.shape[0]
    n_pages = page_table.shape[1]
    page = cache_k_da.shape[2]
    n_pool = cache_k_da.shape[1]
    past = n_pages * page
    wc = BRANCH_W
    n_qkv = N_PROJ * wc

    cos_p, sin_p = _rope_tables(jnp.arange(s))
    cos_s, sin_s = _rope_tables(jnp.full((nseq,), past))
    gfin = norm_final.reshape(1, d)

    xp = x_prompt
    xs = x_sample.reshape(1, nseq, d)
    outs = {k: [] for k in ("pkd", "pvd", "pks", "pvs", "pmk", "pmv", "skd", "svd", "sks", "svs")}
    for l in range(depth):
        lam_init = 0.8 - 0.6 * math.exp(-0.3 * l)
        last = l == depth - 1
        w_qkv = w_in[l, :, :n_qkv].astype(BF16)
        w_gate = w_in[l, :, n_qkv:].astype(BF16)
        w_mem = w_mem_kv[l].astype(BF16)
        wda, wsb, wx = w_br_da[l].astype(BF16), w_br_sb[l].astype(BF16), w_br_x[l].astype(BF16)
        wo, wup, wdn = w_o[l].astype(BF16), w_up[l].astype(BF16), w_down[l].astype(BF16)
        gattn, gmem, gmlp = norm_attn[l].reshape(1, d), norm_mem[l].reshape(1, d), norm_mlp[l].reshape(1, d)
        lam4 = jnp.stack([lambda_q1[l], lambda_k1[l], lambda_q2[l], lambda_k2[l]]).astype(F32)
        gain = subln_gain[l].reshape(1, DA_V)
        tail_w = (gattn, w_gate, wda, wsb, wx, wo, gmlp, wup, wdn, gfin)

        (qda, kda, kdab, vda, vdab, qsb, ksb, ksbb, vsb, vsbb, qx) = _inproj(
            xp, gattn, w_qkv, cos_p, sin_p, _row_tile(s, 512))
        mem_k, mem_kb, mem_v, mem_vb = _memkv(mem_prompt, gmem, w_mem)
        tq = _row_tile(s, ATTN_TILE)
        o_da = _da_attention(lam4, gain.reshape(DA_V, 1), qda, kdab, vdab, lam_init, tq)
        o_sb = _sb_attention(qsb, ksbb, vsbb, tq)
        o_x = _cross_attention(qx, mem_kb, mem_vb, _row_tile(s, 512))
        n = nb * s
        xp = _tail(xp.reshape(n, d), o_da.reshape(n, wc), o_sb.reshape(n, wc), o_x.reshape(n, wc),
                   *tail_w, final_norm=last, tm=_row_tile(n, 256)).reshape(nb, s, d)
        outs["pkd"].append(kda.reshape(nb, s, DA_HEADS, 2, DA_QK))
        outs["pvd"].append(vda.reshape(nb, s, DA_HEADS, DA_V))
        outs["pks"].append(ksb.reshape(nb, s, SB_HEADS, SB_DIM))
        outs["pvs"].append(vsb.reshape(nb, s, SB_HEADS, SB_DIM))
        outs["pmk"].append(mem_k.reshape(nb, -1, X_HEADS, X_DIM))
        outs["pmv"].append(mem_v.reshape(nb, -1, X_HEADS, X_DIM))

        (qda, kda, kdab, vda, vdab, qsb, ksb, ksbb, vsb, vsbb, qx) = _inproj(
            xs, gattn, w_qkv, cos_s, sin_s, nseq)
        as_rows = lambda a: a.reshape(nseq, 1, wc).astype(F32)
        o_da, o_sb, o_x = _decode_attention(
            page_table, lam4, gain, as_rows(qda), as_rows(qsb), as_rows(qx), as_rows(kdab), as_rows(vdab),
            _head_rows(cache_mem_k[l]), _head_rows(cache_mem_v[l]),
            _feature_major(cache_k_da[l]), _head_rows(cache_v_da[l]),
            _feature_major(cache_k_sb[l]), _feature_major(cache_v_sb[l]),
            lam_init, pages_per_step=math.gcd(n_pages, DECODE_PAGES_PER_STEP))
        as_b16 = lambda a: a.reshape(nseq, wc).astype(BF16)
        xs = _tail(xs.reshape(nseq, d), as_b16(o_da), as_b16(o_sb), as_b16(o_x),
                   *tail_w, final_norm=last, tm=nseq).reshape(1, nseq, d)
        outs["skd"].append(kda.reshape(nseq, 1, DA_HEADS, 2, DA_QK))
        outs["svd"].append(vda.reshape(nseq, 1, DA_HEADS, DA_V))
        outs["sks"].append(ksb.reshape(nseq, 1, SB_HEADS, SB_DIM))
        outs["svs"].append(vsb.reshape(nseq, 1, SB_HEADS, SB_DIM))

    st = lambda k: jnp.stack(outs[k])
    return (xp, xs.reshape(nseq, 1, d), st("pkd"), st("pvd"), st("pks"), st("pvs"), st("pmk"), st("pmv"),
            st("skd"), st("svd"), st("sks"), st("svs"))
```

```python
import functools
import math

import jax
import jax.numpy as jnp
from jax import lax
from jax.experimental import pallas as pl
from jax.experimental.pallas import tpu as pltpu

F32 = jnp.float32
BF16 = jnp.bfloat16

EPS = 1e-6
ROPE_THETA = 10000.0
DA_HEADS = 4
DA_QK = 64
DA_V = 2 * DA_QK
SB_HEADS = 8
SB_DIM = 64
X_HEADS = 4
X_DIM = 128
N_BRANCH = 3
BRANCH_W = 512
N_PROJ = 7
LANES = 128
STAT_ROWS = 16
DECODE_PAGES_PER_STEP = 8
SUFFIX_GROUP_PAGES = 4
SUFFIX_SUB_KEYS = 256
ATTN_TILE = 512
DA_HEADS_PER_STEP = 4
SB_PAIRS_PER_STEP = 2
NEG_BIG = -1e30
MIB = 1024 * 1024


def _cparams(semantics, vmem_mib):
    return pltpu.CompilerParams(dimension_semantics=semantics, vmem_limit_bytes=vmem_mib * MIB)


def _rms(x, g):
    return x * lax.rsqrt(jnp.mean(x * x, axis=-1, keepdims=True) + EPS) * g


def _dot(a, b):
    return jnp.dot(a, b, preferred_element_type=F32)


def _dot_nt(a, b):
    return lax.dot_general(a, b, (((1,), (1,)), ((), ())), preferred_element_type=F32)


def _neg_softplus(z):
    return jnp.minimum(-z, 0.0) - jnp.log(1.0 + jnp.exp(-jnp.abs(z)))


def _lam(lam_ref, lam_init):
    lv = lam_ref[...]
    s1 = jnp.sum(lv[0:1] * lv[1:2], axis=-1, keepdims=True)
    s2 = jnp.sum(lv[2:3] * lv[3:4], axis=-1, keepdims=True)
    return jnp.exp(s1) - jnp.exp(s2) + lam_init


def _inproj_kernel(x_ref, g_ref, w_ref, cos_ref, sin_ref,
                   qda_ref, kda_ref, kdab_ref, vda_ref, vdab_ref,
                   qsb_ref, ksb_ref, ksbb_ref, vsb_ref, vsbb_ref, qx_ref):
    w_cols = BRANCH_W
    h = _rms(x_ref[0], g_ref[...]).astype(BF16)

    def proj(c):
        return _dot(h, w_ref[:, c * w_cols:(c + 1) * w_cols])

    cos = cos_ref[...]
    sin = sin_ref[...]
    first = (lax.broadcasted_iota(jnp.int32, (1, w_cols), 1) % DA_QK) < (DA_QK // 2)

    def rope(t):
        partner = jnp.where(first, pltpu.roll(t, w_cols - DA_QK // 2, 1), pltpu.roll(t, DA_QK // 2, 1))
        return t * cos + partner * sin

    qda_ref[0] = (rope(proj(0)) * (DA_QK ** -0.5)).astype(BF16)
    kda = rope(proj(1))
    kda_ref[0] = kda
    kdab_ref[0] = kda.astype(BF16)
    vda = proj(2)
    vda_ref[0] = vda
    vdab_ref[0] = vda.astype(BF16)
    qsb_ref[0] = (proj(3) * (SB_DIM ** -0.5)).astype(BF16)
    ksb = proj(4)
    ksb_ref[0] = ksb
    ksbb_ref[0] = ksb.astype(BF16)
    vsb = proj(5)
    vsb_ref[0] = vsb
    vsbb_ref[0] = vsb.astype(BF16)
    qx_ref[0] = proj(6).astype(BF16)


def _inproj(x, g, w, cos, sin, tm):
    nb, s, d = x.shape
    grid = (s // tm, nb)
    wc = BRANCH_W
    row = lambda si, b: (b, si, 0)
    f32o = jax.ShapeDtypeStruct((nb, s, wc), F32)
    b16o = jax.ShapeDtypeStruct((nb, s, wc), BF16)
    ospec = pl.BlockSpec((1, tm, wc), row)
    return pl.pallas_call(
        _inproj_kernel,
        grid=grid,
        in_specs=[
            pl.BlockSpec((1, tm, d), row),
            pl.BlockSpec((1, d), lambda si, b: (0, 0)),
            pl.BlockSpec((d, N_PROJ * wc), lambda si, b: (0, 0)),
            pl.BlockSpec((tm, wc), lambda si, b: (si, 0)),
            pl.BlockSpec((tm, wc), lambda si, b: (si, 0)),
        ],
        out_specs=[ospec] * 11,
        out_shape=[b16o, f32o, b16o, f32o, b16o, b16o, f32o, b16o, f32o, b16o, b16o],
        compiler_params=_cparams(("arbitrary", "arbitrary"), 48),
        name="inproj",
    )(x, g, w, cos, sin)


def _memkv_kernel(x_ref, g_ref, w_ref, k_ref, kb_ref, v_ref, vb_ref):
    h = _rms(x_ref[0], g_ref[...]).astype(BF16)
    k = _dot(h, w_ref[:, :BRANCH_W])
    v = _dot(h, w_ref[:, BRANCH_W:])
    k_ref[0] = k
    kb_ref[0] = k.astype(BF16)
    v_ref[0] = v
    vb_ref[0] = v.astype(BF16)


def _memkv(mem, g, w):
    nb, nm, d = mem.shape
    wc = BRANCH_W
    row = lambda b: (b, 0, 0)
    f32o = jax.ShapeDtypeStruct((nb, nm, wc), F32)
    b16o = jax.ShapeDtypeStruct((nb, nm, wc), BF16)
    ospec = pl.BlockSpec((1, nm, wc), row)
    return pl.pallas_call(
        _memkv_kernel,
        grid=(nb,),
        in_specs=[
            pl.BlockSpec((1, nm, d), row),
            pl.BlockSpec((1, d), lambda b: (0, 0)),
            pl.BlockSpec((d, 2 * wc), lambda b: (0, 0)),
        ],
        out_specs=[ospec] * 4,
        out_shape=[f32o, b16o, f32o, b16o],
        compiler_params=_cparams(("arbitrary",), 32),
        name="memkv",
    )(mem, g, w)


def _transpose_tiles(src_ref, cols, dst_ref, tk):
    for c in range(dst_ref.shape[0]):
        dst_ref[c] = src_ref[0, c * tk:(c + 1) * tk, cols].astype(F32).T.astype(dst_ref.dtype)


def _da_kernel(lam_ref, gain_ref, q_ref, k_ref, v_ref, o_ref, vt_ref, acc_ref, m_ref, l_ref, *,
               tq, tk, lam_init, heads):
    qi = pl.program_id(2)
    head_cols = [slice(h * DA_V, (h + 1) * DA_V) for h in range(heads)]

    @pl.when(qi == 0)
    def _():
        for h in range(heads):
            _transpose_tiles(v_ref, head_cols[h], vt_ref.at[h], tk)

    lane = lax.broadcasted_iota(jnp.int32, (1, DA_V), 1)
    q2 = []
    for h in range(heads):
        q = q_ref[0, :, head_cols[h]]
        zero = jnp.zeros_like(q)
        q2.append(jnp.concatenate([jnp.where(lane < DA_QK, q, zero), jnp.where(lane >= DA_QK, q, zero)], axis=0))

    m_ref[...] = jnp.full(m_ref.shape, NEG_BIG, F32)
    l_ref[...] = jnp.zeros(l_ref.shape, F32)
    acc_ref[...] = jnp.zeros(acc_ref.shape, F32)

    def tile(j, masked):
        start = pl.multiple_of(j * tk, tk)
        scores = [_dot_nt(k_ref[0, pl.ds(start, tk), head_cols[h]], q2[h]) for h in range(heads)]
        probs, alphas = [], []
        for h in range(heads):
            s = scores[h]
            if masked:
                key = lax.broadcasted_iota(jnp.int32, (tk, 2 * tq), 0)
                query = lax.broadcasted_iota(jnp.int32, (tk, 2 * tq), 1) % tq
                s = jnp.where(key <= query, s, NEG_BIG)
            m_prev = m_ref[h]
            m_new = jnp.maximum(m_prev, jnp.max(s, axis=0, keepdims=True))
            alpha = jnp.exp(m_prev - m_new)
            p = jnp.exp(s - m_new)
            l_ref[h] = alpha * l_ref[h] + jnp.sum(p, axis=0, keepdims=True)
            m_ref[h] = m_new
            probs.append(p.astype(BF16))
            alphas.append(alpha)
        for h in range(heads):
            acc_ref[h] = alphas[h] * acc_ref[h] + _dot(vt_ref[h, j], probs[h])

    def body(j, carry):
        tile(j, False)
        return carry

    lax.fori_loop(0, qi, body, 0)
    tile(qi, True)

    lam = _lam(lam_ref, lam_init)
    for h in range(heads):
        accn = acc_ref[h] / l_ref[h]
        o = accn[:, :tq] - lam * accn[:, tq:]
        o = o * lax.rsqrt(jnp.mean(o * o, axis=0, keepdims=True) + EPS) * gain_ref[...] * (1.0 - lam_init)
        o_ref[0, :, head_cols[h]] = o.T.astype(o_ref.dtype)


def _da_attention(lam4, gain_col, q, k, v, lam_init, tq):
    nb, s, _ = q.shape
    tk = tq
    heads = DA_HEADS_PER_STEP
    kern = functools.partial(_da_kernel, tq=tq, tk=tk, lam_init=lam_init, heads=heads)
    wide = heads * DA_V
    return pl.pallas_call(
        kern,
        grid=(nb, DA_HEADS // heads, s // tq),
        in_specs=[
            pl.BlockSpec((4, DA_QK), lambda b, h, qi: (0, 0)),
            pl.BlockSpec((DA_V, 1), lambda b, h, qi: (0, 0)),
            pl.BlockSpec((1, tq, wide), lambda b, h, qi: (b, qi, h)),
            pl.BlockSpec((1, s, wide), lambda b, h, qi: (b, 0, h)),
            pl.BlockSpec((1, s, wide), lambda b, h, qi: (b, 0, h)),
        ],
        out_specs=pl.BlockSpec((1, tq, wide), lambda b, h, qi: (b, qi, h)),
        out_shape=jax.ShapeDtypeStruct((nb, s, DA_HEADS * DA_V), BF16),
        scratch_shapes=[
            pltpu.VMEM((heads, s // tk, DA_V, tk), BF16),
            pltpu.VMEM((heads, DA_V, 2 * tq), F32),
            pltpu.VMEM((heads, 1, 2 * tq), F32),
            pltpu.VMEM((heads, 1, 2 * tq), F32),
        ],
        compiler_params=_cparams(("arbitrary", "arbitrary", "arbitrary"), 40),
        name="da_attention",
    )(lam4, gain_col, q, k, v)


def _split_bf16(x):
    hi = x.astype(BF16)
    lo = (x - hi.astype(F32)).astype(BF16)
    return hi, lo


def _sb_kernel(q_ref, k_ref, v_ref, o_ref, vt_ref, acc_ref, c_ref, *, tq, tk, pairs):
    qi = pl.program_id(2)
    pair_cols = [slice(p * LANES, (p + 1) * LANES) for p in range(pairs)]

    @pl.when(qi == 0)
    def _():
        for p in range(pairs):
            _transpose_tiles(v_ref, pair_cols[p], vt_ref.at[p], tk)

    lane = lax.broadcasted_iota(jnp.int32, (1, LANES), 1)
    q2 = []
    for p in range(pairs):
        q = q_ref[0, :, pair_cols[p]]
        zero = jnp.zeros_like(q)
        q2.append(jnp.concatenate([jnp.where(lane < SB_DIM, q, zero), jnp.where(lane >= SB_DIM, q, zero)], axis=0))
    sub = math.gcd(tk, SUFFIX_SUB_KEYS)
    newer = jnp.where(lax.broadcasted_iota(jnp.int32, (sub, sub), 1) > lax.broadcasted_iota(jnp.int32, (sub, sub), 0),
                      1.0, 0.0).astype(BF16)

    c_ref[...] = jnp.zeros(c_ref.shape, F32)
    acc_ref[...] = jnp.zeros(acc_ref.shape, F32)

    def tile(j, masked):
        start = pl.multiple_of(j * tk, tk)
        zs = [_dot_nt(k_ref[0, pl.ds(start, tk), pair_cols[p]], q2[p]) for p in range(pairs)]
        if masked:
            key = lax.broadcasted_iota(jnp.int32, (tk, 2 * tq), 0)
            query = lax.broadcasted_iota(jnp.int32, (tk, 2 * tq), 1) % tq
            keep = key < query
        log_betas, rests = [], []
        for p in range(pairs):
            log_1m = _neg_softplus(zs[p])
            log_betas.append(zs[p] + log_1m)
            if masked:
                log_1m = jnp.where(keep, log_1m, 0.0)
            c = c_ref[p]
            blocks = []
            for blk in reversed(range(tk // sub)):
                part = log_1m[blk * sub:(blk + 1) * sub]
                hi, lo = _split_bf16(part)
                blocks.append(_dot(newer, hi) + _dot(newer, lo) + c)
                c = blocks[-1][0:1, :] + part[0:1, :]
            c_ref[p] = c
            rests.append(jnp.concatenate(blocks[::-1], axis=0))
        for p in range(pairs):
            a = jnp.exp(log_betas[p] + rests[p])
            if masked:
                a = jnp.where(keep, a, 0.0)
            a = a.astype(BF16)
            vt = vt_ref[p, j]
            acc_ref[p, :SB_DIM, :] = acc_ref[p, :SB_DIM, :] + _dot(vt[:SB_DIM], a[:, :tq])
            acc_ref[p, SB_DIM:, :] = acc_ref[p, SB_DIM:, :] + _dot(vt[SB_DIM:], a[:, tq:])

    tile(qi, True)

    def body(jj, carry):
        tile(qi - 1 - jj, False)
        return carry

    lax.fori_loop(0, qi, body, 0)
    for p in range(pairs):
        o_ref[0, :, pair_cols[p]] = acc_ref[p].T.astype(o_ref.dtype)


def _sb_attention(q, k, v, tq):
    nb, s, w = q.shape
    tk = tq
    pairs = SB_PAIRS_PER_STEP
    wide = pairs * LANES
    kern = functools.partial(_sb_kernel, tq=tq, tk=tk, pairs=pairs)
    return pl.pallas_call(
        kern,
        grid=(nb, w // wide, s // tq),
        in_specs=[
            pl.BlockSpec((1, tq, wide), lambda b, h, qi: (b, qi, h)),
            pl.BlockSpec((1, s, wide), lambda b, h, qi: (b, 0, h)),
            pl.BlockSpec((1, s, wide), lambda b, h, qi: (b, 0, h)),
        ],
        out_specs=pl.BlockSpec((1, tq, wide), lambda b, h, qi: (b, qi, h)),
        out_shape=jax.ShapeDtypeStruct((nb, s, w), BF16),
        scratch_shapes=[
            pltpu.VMEM((pairs, s // tk, LANES, tk), BF16),
            pltpu.VMEM((pairs, LANES, tq), F32),
            pltpu.VMEM((pairs, 1, 2 * tq), F32),
        ],
        compiler_params=_cparams(("arbitrary", "arbitrary", "arbitrary"), 40),
        name="sb_attention",
    )(q, k, v)


def _cross_kernel(q_ref, k_ref, v_ref, o_ref):
    for h in range(X_HEADS):
        cols = slice(h * X_DIM, (h + 1) * X_DIM)
        s = _dot_nt(q_ref[0, :, cols], k_ref[0, :, cols]) * (X_DIM ** -0.5)
        p = jnp.exp(s - jnp.max(s, axis=-1, keepdims=True))
        p = p / jnp.sum(p, axis=-1, keepdims=True)
        o_ref[0, :, cols] = _dot(p.astype(BF16), v_ref[0, :, cols]).astype(o_ref.dtype)


def _cross_attention(q, mk, mv, tq):
    nb, s, w = q.shape
    nm = mk.shape[1]
    return pl.pallas_call(
        _cross_kernel,
        grid=(nb, s // tq),
        in_specs=[
            pl.BlockSpec((1, tq, w), lambda b, qi: (b, qi, 0)),
            pl.BlockSpec((1, nm, w), lambda b, qi: (b, 0, 0)),
            pl.BlockSpec((1, nm, w), lambda b, qi: (b, 0, 0)),
        ],
        out_specs=pl.BlockSpec((1, tq, w), lambda b, qi: (b, qi, 0)),
        out_shape=jax.ShapeDtypeStruct((nb, s, w), BF16),
        compiler_params=_cparams(("arbitrary", "arbitrary"), 32),
        name="cross_attention",
    )(q, mk, mv)


def _decode_kernel(pt_ref, lam_ref, gain_ref, expand_ref, later_ref, qd_ref, qx_ref, kdn_ref, vdn_ref,
                   qdb_ref, qsb_ref, mk_ref, mv_ref, *rest, pages_per_step, n_steps, lam_init):
    del pt_ref
    npg = pages_per_step
    kd_refs, vd_refs = rest[0:npg], rest[npg:2 * npg]
    ks_refs, vs_refs = rest[2 * npg:3 * npg], rest[3 * npg:4 * npg]
    oda_ref, osb_ref, ox_ref = rest[4 * npg:4 * npg + 3]
    accd_ref, accs_ref, m_ref, l_ref, c_ref = rest[4 * npg + 3:]
    j = pl.program_id(1)
    page = kd_refs[0].shape[1]
    n_groups = BRANCH_W // DA_QK

    def group_scores(kt_ref, qb):
        s8 = jnp.sum((kt_ref[...] * qb).reshape(n_groups, DA_QK, page), axis=1)
        return jnp.concatenate([s8, jnp.zeros((STAT_ROWS - n_groups, page), F32)], axis=0)
    w = BRANCH_W
    row = lax.broadcasted_iota(jnp.int32, (STAT_ROWS, w), 0)
    col = lax.broadcasted_iota(jnp.int32, (STAT_ROWS, w), 1)
    row_v = lax.broadcasted_iota(jnp.int32, (STAT_ROWS, DA_V), 0)
    own_da = (lax.broadcasted_iota(jnp.int32, (npg * STAT_ROWS, w), 1) % DA_HEADS
              == (lax.broadcasted_iota(jnp.int32, (npg * STAT_ROWS, w), 0) % STAT_ROWS) // 2)

    def pick_rows(x, sel):
        return jnp.sum(jnp.where(sel, x, 0.0), axis=0, keepdims=True)

    @pl.when(j == 0)
    def _():
        qd = jnp.where(col // DA_QK == row, qd_ref[0], 0.0)
        m_ref[...] = jnp.sum(qd * kdn_ref[0], axis=-1, keepdims=True)
        l_ref[...] = jnp.ones(l_ref.shape, F32)
        v_new = jnp.zeros((STAT_ROWS, DA_V), F32)
        for h in range(DA_HEADS):
            v_new = jnp.where(row_v // 2 == h, vdn_ref[0][:, h * DA_V:(h + 1) * DA_V], v_new)
        accd_ref[...] = v_new
        accs_ref[...] = jnp.zeros(accs_ref.shape, F32)
        c_ref[...] = jnp.zeros(c_ref.shape, F32)
        nmh = mk_ref.shape[1]
        qx = jnp.zeros((STAT_ROWS, X_DIM), F32)
        for h in range(X_HEADS):
            qx = jnp.where(row_v == h, qx_ref[0][:, h * X_DIM:(h + 1) * X_DIM], qx)
        own_x = (lax.broadcasted_iota(jnp.int32, (STAT_ROWS, nmh), 1) % X_HEADS
                 == lax.broadcasted_iota(jnp.int32, (STAT_ROWS, nmh), 0))
        s = _dot_nt(qx.astype(BF16), mk_ref[0].astype(BF16)) * (X_DIM ** -0.5)
        s = jnp.where(own_x, s, NEG_BIG)
        p = jnp.exp(s - jnp.max(s, axis=-1, keepdims=True))
        p = (p / jnp.sum(p, axis=-1, keepdims=True)).astype(BF16)
        ox = _dot(p, mv_ref[0].astype(BF16))
        for h in range(X_HEADS):
            ox_ref[0, :, h * X_DIM:(h + 1) * X_DIM] = ox[h:h + 1, :]

    qdb = qdb_ref[0]
    s = jnp.concatenate([group_scores(kd_refs[i], qdb) for i in range(npg)], axis=1)
    m_prev = m_ref[...]
    m_new = jnp.maximum(m_prev, jnp.max(s, axis=-1, keepdims=True))
    alpha = jnp.exp(m_prev - m_new)
    p = jnp.exp(s - m_new)
    l_ref[...] = alpha * l_ref[...] + jnp.sum(p, axis=-1, keepdims=True)
    m_ref[...] = m_new
    p = p.astype(BF16)
    p_rows = jnp.concatenate([p[:, i * page:(i + 1) * page] for i in range(npg)], axis=0)
    p2 = jnp.where(own_da, _dot(p_rows, expand_ref[...]), 0.0).astype(BF16)
    acc = alpha * accd_ref[...]
    for i in range(npg):
        acc = acc + _dot(p2[i * STAT_ROWS:(i + 1) * STAT_ROWS], vd_refs[i][...].astype(BF16))
    accd_ref[...] = acc

    qsb = qsb_ref[0]
    group = later_ref.shape[0] // page
    c = c_ref[...]
    acc = accs_ref[...]
    for g in reversed(range(npg // group)):
        pages = range(g * group, (g + 1) * group)
        z = jnp.concatenate([group_scores(ks_refs[i], qsb) for i in pages], axis=1)
        log_1m = _neg_softplus(z)
        hi, lo = _split_bf16(log_1m)
        rest = _dot(jnp.concatenate([hi, lo], axis=0), later_ref[...])
        rest = rest[:STAT_ROWS] + rest[STAT_ROWS:] + c
        a = jnp.exp(z + log_1m + rest).astype(BF16)
        for n, i in enumerate(pages):
            acc = acc + _dot_nt(a[:, n * page:(n + 1) * page], vs_refs[i][...].astype(BF16))
        c = c + jnp.sum(log_1m, axis=-1, keepdims=True)
    accs_ref[...] = acc
    c_ref[...] = c

    @pl.when(j == n_steps - 1)
    def _():
        lam = _lam(lam_ref, lam_init)
        accn = accd_ref[...] / l_ref[...]
        for h in range(DA_HEADS):
            o = accn[2 * h:2 * h + 1, :] - lam * accn[2 * h + 1:2 * h + 2, :]
            oda_ref[0, :, h * DA_V:(h + 1) * DA_V] = _rms(o, gain_ref[...]) * (1.0 - lam_init)
        osb_ref[0] = pick_rows(accs_ref[...], col // SB_DIM == row)


def _decode_attention(page_table, lam4, gain, qd, qs, qx, kdn, vdn, mem_k, mem_v,
                      pool_kd, pool_vd, pool_ks, pool_vs, lam_init, pages_per_step):
    nseq, n_pages = page_table.shape
    w = BRANCH_W
    npg = pages_per_step
    n_steps = n_pages // npg
    page = pool_kd.shape[2]
    pt_flat = page_table.reshape(-1)
    expand = (jnp.arange(page)[:, None] == jnp.arange(page * DA_HEADS)[None, :] // DA_HEADS).astype(BF16)
    group_tokens = jnp.arange(math.gcd(npg, SUFFIX_GROUP_PAGES) * page)
    later = (group_tokens[:, None] > group_tokens[None, :]).astype(BF16)

    seq = lambda b, j, pt: (b, 0, 0)
    const = lambda b, j, pt: (0, 0)

    def page_spec(pool, i):
        return pl.BlockSpec((None,) + pool.shape[1:],
                            lambda b, j, pt, i=i: (pt[b * n_pages + n_pages - (j + 1) * npg + i], 0, 0))

    row_spec = pl.BlockSpec((1, 1, w), seq)
    mem_spec = pl.BlockSpec((1,) + mem_k.shape[1:], seq)
    lanes_of = lambda q: jnp.broadcast_to(q.reshape(nseq, w, 1), (nseq, w, page))
    qb_spec = pl.BlockSpec((1, w, page), seq)
    in_specs = ([pl.BlockSpec((4, DA_QK), const), pl.BlockSpec((1, DA_V), const),
                 pl.BlockSpec(expand.shape, const), pl.BlockSpec(later.shape, const)]
                + [row_spec] * 4 + [qb_spec] * 2 + [mem_spec] * 2
                + [page_spec(pool, i) for pool in (pool_kd, pool_vd, pool_ks, pool_vs) for i in range(npg)])
    out = jax.ShapeDtypeStruct((nseq, 1, w), F32)
    kern = functools.partial(_decode_kernel, pages_per_step=npg, n_steps=n_steps, lam_init=lam_init)
    return pl.pallas_call(
        kern,
        grid_spec=pltpu.PrefetchScalarGridSpec(
            num_scalar_prefetch=1,
            grid=(nseq, n_steps),
            in_specs=in_specs,
            out_specs=[row_spec] * 3,
            scratch_shapes=[
                pltpu.VMEM((STAT_ROWS, DA_V), F32),
                pltpu.VMEM((STAT_ROWS, w), F32),
                pltpu.VMEM((STAT_ROWS, 1), F32),
                pltpu.VMEM((STAT_ROWS, 1), F32),
                pltpu.VMEM((STAT_ROWS, 1), F32),
            ],
        ),
        out_shape=[out, out, out],
        compiler_params=_cparams(("arbitrary", "arbitrary"), 40),
        name="decode_attention",
    )(pt_flat, lam4, gain, expand, later, qd, qx, kdn, vdn, lanes_of(qd), lanes_of(qs), mem_k, mem_v,
      *([pool_kd] * npg), *([pool_vd] * npg), *([pool_ks] * npg), *([pool_vs] * npg))


def _tail_kernel(x_ref, oda_ref, osb_ref, ox_ref, gattn_ref, wg_ref, wda_ref, wsb_ref, wx_ref, wo_ref,
                 gmlp_ref, wup_ref, wdn_ref, gfin_ref, y_ref, *, final_norm, ff_chunk):
    x = x_ref[...]
    d = x.shape[-1]
    h = _rms(x, gattn_ref[...]).astype(BF16)
    m = None
    for i, (o_ref, w_ref) in enumerate(((oda_ref, wda_ref), (osb_ref, wsb_ref), (ox_ref, wx_ref))):
        gate = jax.nn.sigmoid(_dot(h, wg_ref[:, i * d:(i + 1) * d]))
        term = gate * _dot(o_ref[...], w_ref[...])
        m = term if m is None else m + term
    x = x + _dot(m.astype(BF16), wo_ref[...])

    h = _rms(x, gmlp_ref[...]).astype(BF16)
    d_ff = wup_ref.shape[1]
    y = x
    for c in range(d_ff // ff_chunk):
        cols = slice(c * ff_chunk, (c + 1) * ff_chunk)
        u = jnp.square(jnp.maximum(_dot(h, wup_ref[:, cols]), 0.0))
        y = y + _dot(u.astype(BF16), wdn_ref[cols, :])
    if final_norm:
        y = _rms(y, gfin_ref[...])
    y_ref[...] = y


def _tail(x, oda, osb, ox, gattn, wg, wda, wsb, wx, wo, gmlp, wup, wdn, gfin, final_norm, tm):
    n, d = x.shape
    wc = BRANCH_W
    d_ff = wup.shape[1]
    row = lambda i: (i, 0)
    const = lambda i: (0, 0)

    def resident(shape):
        return pl.BlockSpec(shape, const, pipeline_mode=pl.Buffered(1))

    kern = functools.partial(_tail_kernel, final_norm=final_norm, ff_chunk=1024)
    return pl.pallas_call(
        kern,
        grid=(n // tm,),
        in_specs=[
            pl.BlockSpec((tm, d), row),
            pl.BlockSpec((tm, wc), row),
            pl.BlockSpec((tm, wc), row),
            pl.BlockSpec((tm, wc), row),
            resident((1, d)),
            resident((d, N_BRANCH * d)),
            resident((wc, d)),
            resident((wc, d)),
            resident((wc, d)),
            resident((d, d)),
            resident((1, d)),
            resident((d, d_ff)),
            resident((d_ff, d)),
            resident((1, d)),
        ],
        out_specs=pl.BlockSpec((tm, d), row),
        out_shape=jax.ShapeDtypeStruct((n, d), F32),
        compiler_params=_cparams(("arbitrary",), 56),
        name="tail",
    )(x, oda, osb, ox, gattn, wg, wda, wsb, wx, wo, gmlp, wup, wdn, gfin)


def _rope_tables(pos):
    half = DA_QK // 2
    freqs = ROPE_THETA ** (-jnp.arange(0, DA_QK, 2, dtype=F32) / DA_QK)
    ang = pos.astype(F32)[:, None] * freqs[None, :]
    cos, sin = jnp.cos(ang), jnp.sin(ang)
    reps = BRANCH_W // DA_QK
    cos_t = jnp.tile(jnp.concatenate([cos, cos], axis=-1), (1, reps))
    sin_t = jnp.tile(jnp.concatenate([-sin, sin], axis=-1), (1, reps))
    del half
    return cos_t, sin_t


def _feature_major(pool):
    perm = (0,) + tuple(range(2, pool.ndim)) + (1,)
    return jnp.transpose(pool, perm).reshape(pool.shape[0], BRANCH_W, pool.shape[1])


def _head_rows(a):
    return a.reshape(a.shape[0], a.shape[1] * a.shape[2], a.shape[3])


def _row_tile(n, want):
    return want if n % want == 0 else n


def kernel(x_prompt, x_sample, mem_prompt, cache_k_da, cache_v_da, cache_k_sb, cache_v_sb, cache_mem_k, cache_mem_v, page_table, norm_attn, norm_mem, w_in, w_mem_kv, lambda_q1, lambda_k1, lambda_q2, lambda_k2, subln_gain, w_br_da, w_br_sb, w_br_x, w_o, norm_mlp, w_up, w_down, norm_final):
    nb, s, d = x_prompt.shape
    nseq, t_new, _ = x_sample.shape
    assert t_new == 1, "the decode kernel handles one new token per sequence"
    depth = w_in.shape[0]
    n_pages = page_table.shape[1]
    page = cache_k_da.shape[2]
    n_pool = cache_k_da.shape[1]
    past = n_pages * page
    wc = BRANCH_W
    n_qkv = N_PROJ * wc

    cos_p, sin_p = _rope_tables(jnp.arange(s))
    cos_s, sin_s = _rope_tables(jnp.full((nseq,), past))
    gfin = norm_final.reshape(1, d)

    xp = x_prompt
    xs = x_sample.reshape(1, nseq, d)
    outs = {k: [] for k in ("pkd", "pvd", "pks", "pvs", "pmk", "pmv", "skd", "svd", "sks", "svs")}
    for l in range(depth):
        lam_init = 0.8 - 0.6 * math.exp(-0.3 * l)
        last = l == depth - 1
        w_qkv = w_in[l, :, :n_qkv].astype(BF16)
        w_gate = w_in[l, :, n_qkv:].astype(BF16)
        w_mem = w_mem_kv[l].astype(BF16)
        wda, wsb, wx = w_br_da[l].astype(BF16), w_br_sb[l].astype(BF16), w_br_x[l].astype(BF16)
        wo, wup, wdn = w_o[l].astype(BF16), w_up[l].astype(BF16), w_down[l].astype(BF16)
        gattn, gmem, gmlp = norm_attn[l].reshape(1, d), norm_mem[l].reshape(1, d), norm_mlp[l].reshape(1, d)
        lam4 = jnp.stack([lambda_q1[l], lambda_k1[l], lambda_q2[l], lambda_k2[l]]).astype(F32)
        gain = subln_gain[l].reshape(1, DA_V)
        tail_w = (gattn, w_gate, wda, wsb, wx, wo, gmlp, wup, wdn, gfin)

        (qda, kda, kdab, vda, vdab, qsb, ksb, ksbb, vsb, vsbb, qx) = _inproj(
            xp, gattn, w_qkv, cos_p, sin_p, _row_tile(s, 512))
        mem_k, mem_kb, mem_v, mem_vb = _memkv(mem_prompt, gmem, w_mem)
        tq = _row_tile(s, ATTN_TILE)
        o_da = _da_attention(lam4, gain.reshape(DA_V, 1), qda, kdab, vdab, lam_init, tq)
        o_sb = _sb_attention(qsb, ksbb, vsbb, tq)
        o_x = _cross_attention(qx, mem_kb, mem_vb, _row_tile(s, 512))
        n = nb * s
        xp = _tail(xp.reshape(n, d), o_da.reshape(n, wc), o_sb.reshape(n, wc), o_x.reshape(n, wc),
                   *tail_w, final_norm=last, tm=_row_tile(n, 256)).reshape(nb, s, d)
        outs["pkd"].append(kda.reshape(nb, s, DA_HEADS, 2, DA_QK))
        outs["pvd"].append(vda.reshape(nb, s, DA_HEADS, DA_V))
        outs["pks"].append(ksb.reshape(nb, s, SB_HEADS, SB_DIM))
        outs["pvs"].append(vsb.reshape(nb, s, SB_HEADS, SB_DIM))
        outs["pmk"].append(mem_k.reshape(nb, -1, X_HEADS, X_DIM))
        outs["pmv"].append(mem_v.reshape(nb, -1, X_HEADS, X_DIM))

        (qda, kda, kdab, vda, vdab, qsb, ksb, ksbb, vsb, vsbb, qx) = _inproj(
            xs, gattn, w_qkv, cos_s, sin_s, nseq)
        as_rows = lambda a: a.reshape(nseq, 1, wc).astype(F32)
        o_da, o_sb, o_x = _decode_attention(
            page_table, lam4, gain, as_rows(qda), as_rows(qsb), as_rows(qx), as_rows(kdab), as_rows(vdab),
            _head_rows(cache_mem_k[l]), _head_rows(cache_mem_v[l]),
            _feature_major(cache_k_da[l]), _head_rows(cache_v_da[l]),
            _feature_major(cache_k_sb[l]), _feature_major(cache_v_sb[l]),
            lam_init, pages_per_step=math.gcd(n_pages, DECODE_PAGES_PER_STEP))
        as_b16 = lambda a: a.reshape(nseq, wc).astype(BF16)
        xs = _tail(xs.reshape(nseq, d), as_b16(o_da), as_b16(o_sb), as_b16(o_x),
                   *tail_w, final_norm=last, tm=nseq).reshape(1, nseq, d)
        outs["skd"].append(kda.reshape(nseq, 1, DA_HEADS, 2, DA_QK))
        outs["svd"].append(vda.reshape(nseq, 1, DA_HEADS, DA_V))
        outs["sks"].append(ksb.reshape(nseq, 1, SB_HEADS, SB_DIM))
        outs["svs"].append(vsb.reshape(nseq, 1, SB_HEADS, SB_DIM))

    st = lambda k: jnp.stack(outs[k])
    return (xp, xs.reshape(nseq, 1, d), st("pkd"), st("pvd"), st("pks"), st("pvs"), st("pmk"), st("pmv"),
            st("skd"), st("svd"), st("sks"), st("svs"))
```

```python
import functools
import math

import jax
import jax.numpy as jnp
from jax import lax
from jax.experimental import pallas as pl
from jax.experimental.pallas import tpu as pltpu

F32 = jnp.float32
BF16 = jnp.bfloat16

EPS = 1e-6
ROPE_THETA = 10000.0
DA_HEADS = 4
DA_QK = 64
DA_V = 2 * DA_QK
SB_HEADS = 8
SB_DIM = 64
X_HEADS = 4
X_DIM = 128
N_BRANCH = 3
BRANCH_W = 512
N_PROJ = 7
LANES = 128
STAT_ROWS = 16
DECODE_PAGES_PER_STEP = 16
SUFFIX_GROUP_PAGES = 4
SUFFIX_SUB_KEYS = 256
ATTN_TILE = 512
DA_HEADS_PER_STEP = 4
SB_PAIRS_PER_STEP = 2
NEG_BIG = -1e30
MIB = 1024 * 1024


def _cparams(semantics, vmem_mib):
    return pltpu.CompilerParams(dimension_semantics=semantics, vmem_limit_bytes=vmem_mib * MIB)


def _rms(x, g):
    return x * lax.rsqrt(jnp.mean(x * x, axis=-1, keepdims=True) + EPS) * g


def _dot(a, b):
    return jnp.dot(a, b, preferred_element_type=F32)


def _dot_nt(a, b):
    return lax.dot_general(a, b, (((1,), (1,)), ((), ())), preferred_element_type=F32)


def _neg_softplus(z):
    nz = -z
    return jnp.minimum(nz, 0.0) - jnp.log(1.0 + jnp.exp(jnp.minimum(z, nz)))


def _lam(lam_ref, lam_init):
    lv = lam_ref[...]
    s1 = jnp.sum(lv[0:1] * lv[1:2], axis=-1, keepdims=True)
    s2 = jnp.sum(lv[2:3] * lv[3:4], axis=-1, keepdims=True)
    return jnp.exp(s1) - jnp.exp(s2) + lam_init


def _inproj_kernel(x_ref, g_ref, w_ref, cos_ref, sin_ref,
                   qda_ref, kda_ref, kdab_ref, vda_ref, vdab_ref,
                   qsb_ref, ksb_ref, ksbb_ref, vsb_ref, vsbb_ref, qx_ref):
    w_cols = BRANCH_W
    h = _rms(x_ref[0], g_ref[...]).astype(BF16)

    def proj(c):
        return _dot(h, w_ref[:, c * w_cols:(c + 1) * w_cols])

    cos = cos_ref[...]
    sin = sin_ref[...]
    first = (lax.broadcasted_iota(jnp.int32, (1, w_cols), 1) % DA_QK) < (DA_QK // 2)

    def rope(t):
        partner = jnp.where(first, pltpu.roll(t, w_cols - DA_QK // 2, 1), pltpu.roll(t, DA_QK // 2, 1))
        return t * cos + partner * sin

    qda_ref[0] = (rope(proj(0)) * (DA_QK ** -0.5)).astype(BF16)
    kda = rope(proj(1))
    kda_ref[0] = kda
    kdab_ref[0] = kda.astype(BF16)
    vda = proj(2)
    vda_ref[0] = vda
    vdab_ref[0] = vda.astype(BF16)
    qsb_ref[0] = (proj(3) * (SB_DIM ** -0.5)).astype(BF16)
    ksb = proj(4)
    ksb_ref[0] = ksb
    ksbb_ref[0] = ksb.astype(BF16)
    vsb = proj(5)
    vsb_ref[0] = vsb
    vsbb_ref[0] = vsb.astype(BF16)
    qx_ref[0] = proj(6).astype(BF16)


def _inproj(x, g, w, cos, sin, tm):
    nb, s, d = x.shape
    grid = (s // tm, nb)
    wc = BRANCH_W
    row = lambda si, b: (b, si, 0)
    f32o = jax.ShapeDtypeStruct((nb, s, wc), F32)
    b16o = jax.ShapeDtypeStruct((nb, s, wc), BF16)
    ospec = pl.BlockSpec((1, tm, wc), row)
    return pl.pallas_call(
        _inproj_kernel,
        grid=grid,
        in_specs=[
            pl.BlockSpec((1, tm, d), row),
            pl.BlockSpec((1, d), lambda si, b: (0, 0)),
            pl.BlockSpec((d, N_PROJ * wc), lambda si, b: (0, 0)),
            pl.BlockSpec((tm, wc), lambda si, b: (si, 0)),
            pl.BlockSpec((tm, wc), lambda si, b: (si, 0)),
        ],
        out_specs=[ospec] * 11,
        out_shape=[b16o, f32o, b16o, f32o, b16o, b16o, f32o, b16o, f32o, b16o, b16o],
        compiler_params=_cparams(("arbitrary", "arbitrary"), 48),
        name="inproj",
    )(x, g, w, cos, sin)


def _memkv_kernel(x_ref, g_ref, w_ref, k_ref, kb_ref, v_ref, vb_ref):
    h = _rms(x_ref[0], g_ref[...]).astype(BF16)
    k = _dot(h, w_ref[:, :BRANCH_W])
    v = _dot(h, w_ref[:, BRANCH_W:])
    k_ref[0] = k
    kb_ref[0] = k.astype(BF16)
    v_ref[0] = v
    vb_ref[0] = v.astype(BF16)


def _memkv(mem, g, w):
    nb, nm, d = mem.shape
    wc = BRANCH_W
    row = lambda b: (b, 0, 0)
    f32o = jax.ShapeDtypeStruct((nb, nm, wc), F32)
    b16o = jax.ShapeDtypeStruct((nb, nm, wc), BF16)
    ospec = pl.BlockSpec((1, nm, wc), row)
    return pl.pallas_call(
        _memkv_kernel,
        grid=(nb,),
        in_specs=[
            pl.BlockSpec((1, nm, d), row),
            pl.BlockSpec((1, d), lambda b: (0, 0)),
            pl.BlockSpec((d, 2 * wc), lambda b: (0, 0)),
        ],
        out_specs=[ospec] * 4,
        out_shape=[f32o, b16o, f32o, b16o],
        compiler_params=_cparams(("arbitrary",), 32),
        name="memkv",
    )(mem, g, w)


def _transpose_tiles(src_ref, cols, dst_ref, tk):
    for c in range(dst_ref.shape[0]):
        dst_ref[c] = src_ref[0, c * tk:(c + 1) * tk, cols].astype(F32).T.astype(dst_ref.dtype)


def _da_kernel(lam_ref, gain_ref, q_ref, k_ref, v_ref, o_ref, vt_ref, acc_ref, m_ref, l_ref, *,
               tq, tk, lam_init, heads):
    qi = pl.program_id(2)
    head_cols = [slice(h * DA_V, (h + 1) * DA_V) for h in range(heads)]

    @pl.when(qi == 0)
    def _():
        for h in range(heads):
            _transpose_tiles(v_ref, head_cols[h], vt_ref.at[h], tk)

    lane = lax.broadcasted_iota(jnp.int32, (1, DA_V), 1)
    q2 = []
    for h in range(heads):
        q = q_ref[0, :, head_cols[h]]
        zero = jnp.zeros_like(q)
        q2.append(jnp.concatenate([jnp.where(lane < DA_QK, q, zero), jnp.where(lane >= DA_QK, q, zero)], axis=0))

    m_ref[...] = jnp.full(m_ref.shape, NEG_BIG, F32)
    l_ref[...] = jnp.zeros(l_ref.shape, F32)
    acc_ref[...] = jnp.zeros(acc_ref.shape, F32)

    def tile(j, masked):
        start = pl.multiple_of(j * tk, tk)
        scores = [_dot_nt(k_ref[0, pl.ds(start, tk), head_cols[h]], q2[h]) for h in range(heads)]
        probs, alphas = [], []
        for h in range(heads):
            s = scores[h]
            if masked:
                key = lax.broadcasted_iota(jnp.int32, (tk, 2 * tq), 0)
                query = lax.broadcasted_iota(jnp.int32, (tk, 2 * tq), 1) % tq
                s = jnp.where(key <= query, s, NEG_BIG)
            m_prev = m_ref[h]
            m_new = jnp.maximum(m_prev, jnp.max(s, axis=0, keepdims=True))
            alpha = jnp.exp(m_prev - m_new)
            p = jnp.exp(s - m_new)
            l_ref[h] = alpha * l_ref[h] + jnp.sum(p, axis=0, keepdims=True)
            m_ref[h] = m_new
            probs.append(p.astype(BF16))
            alphas.append(alpha)
        for h in range(heads):
            acc_ref[h] = alphas[h] * acc_ref[h] + _dot(vt_ref[h, j], probs[h])

    def body(j, carry):
        tile(j, False)
        return carry

    lax.fori_loop(0, qi, body, 0)
    tile(qi, True)

    lam = _lam(lam_ref, lam_init)
    for h in range(heads):
        accn = acc_ref[h] / l_ref[h]
        o = accn[:, :tq] - lam * accn[:, tq:]
        o = o * lax.rsqrt(jnp.mean(o * o, axis=0, keepdims=True) + EPS) * gain_ref[...] * (1.0 - lam_init)
        o_ref[0, :, head_cols[h]] = o.T.astype(o_ref.dtype)


def _da_attention(lam4, gain_col, q, k, v, lam_init, tq):
    nb, s, _ = q.shape
    tk = tq
    heads = DA_HEADS_PER_STEP
    kern = functools.partial(_da_kernel, tq=tq, tk=tk, lam_init=lam_init, heads=heads)
    wide = heads * DA_V
    return pl.pallas_call(
        kern,
        grid=(nb, DA_HEADS // heads, s // tq),
        in_specs=[
            pl.BlockSpec((4, DA_QK), lambda b, h, qi: (0, 0)),
            pl.BlockSpec((DA_V, 1), lambda b, h, qi: (0, 0)),
            pl.BlockSpec((1, tq, wide), lambda b, h, qi: (b, qi, h)),
            pl.BlockSpec((1, s, wide), lambda b, h, qi: (b, 0, h)),
            pl.BlockSpec((1, s, wide), lambda b, h, qi: (b, 0, h)),
        ],
        out_specs=pl.BlockSpec((1, tq, wide), lambda b, h, qi: (b, qi, h)),
        out_shape=jax.ShapeDtypeStruct((nb, s, DA_HEADS * DA_V), BF16),
        scratch_shapes=[
            pltpu.VMEM((heads, s // tk, DA_V, tk), BF16),
            pltpu.VMEM((heads, DA_V, 2 * tq), F32),
            pltpu.VMEM((heads, 1, 2 * tq), F32),
            pltpu.VMEM((heads, 1, 2 * tq), F32),
        ],
        compiler_params=_cparams(("arbitrary", "arbitrary", "arbitrary"), 40),
        name="da_attention",
    )(lam4, gain_col, q, k, v)


def _split_bf16(x):
    hi = x.astype(BF16)
    lo = (x - hi.astype(F32)).astype(BF16)
    return hi, lo


def _sb_kernel(q_ref, k_ref, v_ref, o_ref, vt_ref, acc_ref, c_ref, *, tq, tk, pairs):
    qi = pl.program_id(2)
    pair_cols = [slice(p * LANES, (p + 1) * LANES) for p in range(pairs)]

    @pl.when(qi == 0)
    def _():
        for p in range(pairs):
            _transpose_tiles(v_ref, pair_cols[p], vt_ref.at[p], tk)

    lane = lax.broadcasted_iota(jnp.int32, (1, LANES), 1)
    q2 = []
    for p in range(pairs):
        q = q_ref[0, :, pair_cols[p]]
        zero = jnp.zeros_like(q)
        q2.append(jnp.concatenate([jnp.where(lane < SB_DIM, q, zero), jnp.where(lane >= SB_DIM, q, zero)], axis=0))
    sub = math.gcd(tk, SUFFIX_SUB_KEYS)
    newer = jnp.where(lax.broadcasted_iota(jnp.int32, (sub, sub), 1) > lax.broadcasted_iota(jnp.int32, (sub, sub), 0),
                      1.0, 0.0).astype(BF16)

    c_ref[...] = jnp.zeros(c_ref.shape, F32)
    acc_ref[...] = jnp.zeros(acc_ref.shape, F32)

    def tile(j, masked):
        start = pl.multiple_of(j * tk, tk)
        zs = [_dot_nt(k_ref[0, pl.ds(start, tk), pair_cols[p]], q2[p]) for p in range(pairs)]
        if masked:
            key = lax.broadcasted_iota(jnp.int32, (tk, 2 * tq), 0)
            query = lax.broadcasted_iota(jnp.int32, (tk, 2 * tq), 1) % tq
            keep = key < query
        log_betas, rests = [], []
        for p in range(pairs):
            log_1m = _neg_softplus(zs[p])
            log_betas.append(zs[p] + log_1m)
            if masked:
                log_1m = jnp.where(keep, log_1m, 0.0)
            c = c_ref[p]
            blocks = []
            for blk in reversed(range(tk // sub)):
                part = log_1m[blk * sub:(blk + 1) * sub]
                hi, lo = _split_bf16(part)
                blocks.append(_dot(newer, hi) + _dot(newer, lo) + c)
                c = blocks[-1][0:1, :] + part[0:1, :]
            c_ref[p] = c
            rests.append(jnp.concatenate(blocks[::-1], axis=0))
        for p in range(pairs):
            a = jnp.exp(log_betas[p] + rests[p])
            if masked:
                a = jnp.where(keep, a, 0.0)
            a = a.astype(BF16)
            vt = vt_ref[p, j]
            acc_ref[p, :SB_DIM, :] = acc_ref[p, :SB_DIM, :] + _dot(vt[:SB_DIM], a[:, :tq])
            acc_ref[p, SB_DIM:, :] = acc_ref[p, SB_DIM:, :] + _dot(vt[SB_DIM:], a[:, tq:])

    tile(qi, True)

    def body(jj, carry):
        tile(qi - 1 - jj, False)
        return carry

    lax.fori_loop(0, qi, body, 0)
    for p in range(pairs):
        o_ref[0, :, pair_cols[p]] = acc_ref[p].T.astype(o_ref.dtype)


def _sb_attention(q, k, v, tq):
    nb, s, w = q.shape
    tk = tq
    pairs = SB_PAIRS_PER_STEP
    wide = pairs * LANES
    kern = functools.partial(_sb_kernel, tq=tq, tk=tk, pairs=pairs)
    return pl.pallas_call(
        kern,
        grid=(nb, w // wide, s // tq),
        in_specs=[
            pl.BlockSpec((1, tq, wide), lambda b, h, qi: (b, qi, h)),
            pl.BlockSpec((1, s, wide), lambda b, h, qi: (b, 0, h)),
            pl.BlockSpec((1, s, wide), lambda b, h, qi: (b, 0, h)),
        ],
        out_specs=pl.BlockSpec((1, tq, wide), lambda b, h, qi: (b, qi, h)),
        out_shape=jax.ShapeDtypeStruct((nb, s, w), BF16),
        scratch_shapes=[
            pltpu.VMEM((pairs, s // tk, LANES, tk), BF16),
            pltpu.VMEM((pairs, LANES, tq), F32),
            pltpu.VMEM((pairs, 1, 2 * tq), F32),
        ],
        compiler_params=_cparams(("arbitrary", "arbitrary", "arbitrary"), 40),
        name="sb_attention",
    )(q, k, v)


def _cross_kernel(q_ref, k_ref, v_ref, o_ref):
    for h in range(X_HEADS):
        cols = slice(h * X_DIM, (h + 1) * X_DIM)
        s = _dot_nt(q_ref[0, :, cols], k_ref[0, :, cols]) * (X_DIM ** -0.5)
        p = jnp.exp(s - jnp.max(s, axis=-1, keepdims=True))
        p = p / jnp.sum(p, axis=-1, keepdims=True)
        o_ref[0, :, cols] = _dot(p.astype(BF16), v_ref[0, :, cols]).astype(o_ref.dtype)


def _cross_attention(q, mk, mv, tq):
    nb, s, w = q.shape
    nm = mk.shape[1]
    return pl.pallas_call(
        _cross_kernel,
        grid=(nb, s // tq),
        in_specs=[
            pl.BlockSpec((1, tq, w), lambda b, qi: (b, qi, 0)),
            pl.BlockSpec((1, nm, w), lambda b, qi: (b, 0, 0)),
            pl.BlockSpec((1, nm, w), lambda b, qi: (b, 0, 0)),
        ],
        out_specs=pl.BlockSpec((1, tq, w), lambda b, qi: (b, qi, 0)),
        out_shape=jax.ShapeDtypeStruct((nb, s, w), BF16),
        compiler_params=_cparams(("arbitrary", "arbitrary"), 32),
        name="cross_attention",
    )(q, mk, mv)


def _decode_kernel(pt_ref, lam_ref, gain_ref, expand_ref, later_ref, qd_ref, qx_ref, kdn_ref, vdn_ref,
                   qdb_ref, qsb_ref, mk_ref, mv_ref, *rest, pages_per_step, n_steps, lam_init):
    del pt_ref
    npg = pages_per_step
    kd_refs, vd_refs = rest[0:npg], rest[npg:2 * npg]
    ks_refs, vs_refs = rest[2 * npg:3 * npg], rest[3 * npg:4 * npg]
    oda_ref, osb_ref, ox_ref = rest[4 * npg:4 * npg + 3]
    accd_ref, accs_ref, m_ref, l_ref, c_ref = rest[4 * npg + 3:]
    j = pl.program_id(1)
    page = kd_refs[0].shape[1]
    n_groups = BRANCH_W // DA_QK

    def group_scores(kt_ref, qb):
        s8 = jnp.sum((kt_ref[...] * qb).reshape(n_groups, DA_QK, page), axis=1)
        return jnp.concatenate([s8, jnp.zeros((STAT_ROWS - n_groups, page), F32)], axis=0)
    w = BRANCH_W
    row = lax.broadcasted_iota(jnp.int32, (STAT_ROWS, w), 0)
    col = lax.broadcasted_iota(jnp.int32, (STAT_ROWS, w), 1)
    row_v = lax.broadcasted_iota(jnp.int32, (STAT_ROWS, DA_V), 0)
    own_da = (lax.broadcasted_iota(jnp.int32, (npg * STAT_ROWS, w), 1) % DA_HEADS
              == (lax.broadcasted_iota(jnp.int32, (npg * STAT_ROWS, w), 0) % STAT_ROWS) // 2)

    def pick_rows(x, sel):
        return jnp.sum(jnp.where(sel, x, 0.0), axis=0, keepdims=True)

    @pl.when(j == 0)
    def _():
        qd = jnp.where(col // DA_QK == row, qd_ref[0], 0.0)
        m_ref[...] = jnp.sum(qd * kdn_ref[0], axis=-1, keepdims=True)
        l_ref[...] = jnp.ones(l_ref.shape, F32)
        v_new = jnp.zeros((STAT_ROWS, DA_V), F32)
        for h in range(DA_HEADS):
            v_new = jnp.where(row_v // 2 == h, vdn_ref[0][:, h * DA_V:(h + 1) * DA_V], v_new)
        accd_ref[...] = v_new
        accs_ref[...] = jnp.zeros(accs_ref.shape, F32)
        c_ref[...] = jnp.zeros(c_ref.shape, F32)
        nmh = mk_ref.shape[1]
        qx = jnp.zeros((STAT_ROWS, X_DIM), F32)
        for h in range(X_HEADS):
            qx = jnp.where(row_v == h, qx_ref[0][:, h * X_DIM:(h + 1) * X_DIM], qx)
        own_x = (lax.broadcasted_iota(jnp.int32, (STAT_ROWS, nmh), 1) % X_HEADS
                 == lax.broadcasted_iota(jnp.int32, (STAT_ROWS, nmh), 0))
        s = _dot_nt(qx.astype(BF16), mk_ref[0].astype(BF16)) * (X_DIM ** -0.5)
        s = jnp.where(own_x, s, NEG_BIG)
        p = jnp.exp(s - jnp.max(s, axis=-1, keepdims=True))
        p = (p / jnp.sum(p, axis=-1, keepdims=True)).astype(BF16)
        ox = _dot(p, mv_ref[0].astype(BF16))
        for h in range(X_HEADS):
            ox_ref[0, :, h * X_DIM:(h + 1) * X_DIM] = ox[h:h + 1, :]

    qdb = qdb_ref[0].astype(F32)
    s = jnp.concatenate([group_scores(kd_refs[i], qdb) for i in range(npg)], axis=1)
    m_prev = m_ref[...]
    m_new = jnp.maximum(m_prev, jnp.max(s, axis=-1, keepdims=True))
    alpha = jnp.exp(m_prev - m_new)
    p = jnp.exp(s - m_new)
    l_ref[...] = alpha * l_ref[...] + jnp.sum(p, axis=-1, keepdims=True)
    m_ref[...] = m_new
    p = p.astype(BF16)
    p_rows = jnp.concatenate([p[:, i * page:(i + 1) * page] for i in range(npg)], axis=0)
    p2 = jnp.where(own_da, _dot(p_rows, expand_ref[...]), 0.0).astype(BF16)
    acc = alpha * accd_ref[...]
    for i in range(npg):
        acc = acc + _dot(p2[i * STAT_ROWS:(i + 1) * STAT_ROWS], vd_refs[i][...].astype(BF16))
    accd_ref[...] = acc

    qsb = qsb_ref[0].astype(F32)
    group = later_ref.shape[0] // page
    c = c_ref[...]
    acc = accs_ref[...]
    for g in reversed(range(npg // group)):
        pages = range(g * group, (g + 1) * group)
        z = jnp.concatenate([group_scores(ks_refs[i], qsb) for i in pages], axis=1)
        log_1m = _neg_softplus(z)
        hi, lo = _split_bf16(log_1m)
        rest = _dot(jnp.concatenate([hi, lo], axis=0), later_ref[...])
        rest = rest[:STAT_ROWS] + rest[STAT_ROWS:] + c
        a = jnp.exp(z + log_1m + rest).astype(BF16)
        for n, i in enumerate(pages):
            acc = acc + _dot_nt(a[:, n * page:(n + 1) * page], vs_refs[i][...].astype(BF16))
        c = c + jnp.sum(log_1m, axis=-1, keepdims=True)
    accs_ref[...] = acc
    c_ref[...] = c

    @pl.when(j == n_steps - 1)
    def _():
        lam = _lam(lam_ref, lam_init)
        accn = accd_ref[...] / l_ref[...]
        for h in range(DA_HEADS):
            o = accn[2 * h:2 * h + 1, :] - lam * accn[2 * h + 1:2 * h + 2, :]
            oda_ref[0, :, h * DA_V:(h + 1) * DA_V] = _rms(o, gain_ref[...]) * (1.0 - lam_init)
        osb_ref[0] = pick_rows(accs_ref[...], col // SB_DIM == row)


def _decode_attention(page_table, lam4, gain, qd, qs, qx, kdn, vdn, mem_k, mem_v,
                      pool_kd, pool_vd, pool_ks, pool_vs, lam_init, pages_per_step):
    nseq, n_pages = page_table.shape
    w = BRANCH_W
    npg = pages_per_step
    n_steps = n_pages // npg
    page = pool_kd.shape[2]
    pt_flat = page_table.reshape(-1)
    expand = (jnp.arange(page)[:, None] == jnp.arange(page * DA_HEADS)[None, :] // DA_HEADS).astype(BF16)
    group_tokens = jnp.arange(math.gcd(npg, SUFFIX_GROUP_PAGES) * page)
    later = (group_tokens[:, None] > group_tokens[None, :]).astype(BF16)

    seq = lambda b, j, pt: (b, 0, 0)
    const = lambda b, j, pt: (0, 0)

    def page_spec(pool, i):
        return pl.BlockSpec((None,) + pool.shape[1:],
                            lambda b, j, pt, i=i: (pt[b * n_pages + n_pages - (j + 1) * npg + i], 0, 0))

    row_spec = pl.BlockSpec((1, 1, w), seq)
    mem_spec = pl.BlockSpec((1,) + mem_k.shape[1:], seq)
    lanes_of = lambda q: jnp.broadcast_to(q.reshape(nseq, w, 1).astype(BF16), (nseq, w, page))
    qb_spec = pl.BlockSpec((1, w, page), seq)
    in_specs = ([pl.BlockSpec((4, DA_QK), const), pl.BlockSpec((1, DA_V), const),
                 pl.BlockSpec(expand.shape, const), pl.BlockSpec(later.shape, const)]
                + [row_spec] * 4 + [qb_spec] * 2 + [mem_spec] * 2
                + [page_spec(pool, i) for pool in (pool_kd, pool_vd, pool_ks, pool_vs) for i in range(npg)])
    out = jax.ShapeDtypeStruct((nseq, 1, w), F32)
    kern = functools.partial(_decode_kernel, pages_per_step=npg, n_steps=n_steps, lam_init=lam_init)
    return pl.pallas_call(
        kern,
        grid_spec=pltpu.PrefetchScalarGridSpec(
            num_scalar_prefetch=1,
            grid=(nseq, n_steps),
            in_specs=in_specs,
            out_specs=[row_spec] * 3,
            scratch_shapes=[
                pltpu.VMEM((STAT_ROWS, DA_V), F32),
                pltpu.VMEM((STAT_ROWS, w), F32),
                pltpu.VMEM((STAT_ROWS, 1), F32),
                pltpu.VMEM((STAT_ROWS, 1), F32),
                pltpu.VMEM((STAT_ROWS, 1), F32),
            ],
        ),
        out_shape=[out, out, out],
        compiler_params=_cparams(("arbitrary", "arbitrary"), 40),
        name="decode_attention",
    )(pt_flat, lam4, gain, expand, later, qd, qx, kdn, vdn, lanes_of(qd), lanes_of(qs), mem_k, mem_v,
      *([pool_kd] * npg), *([pool_vd] * npg), *([pool_ks] * npg), *([pool_vs] * npg))


def _tail_kernel(x_ref, oda_ref, osb_ref, ox_ref, gattn_ref, wg_ref, wda_ref, wsb_ref, wx_ref, wo_ref,
                 gmlp_ref, wup_ref, wdn_ref, gfin_ref, y_ref, *, final_norm, ff_chunk):
    x = x_ref[...]
    d = x.shape[-1]
    h = _rms(x, gattn_ref[...]).astype(BF16)
    m = None
    for i, (o_ref, w_ref) in enumerate(((oda_ref, wda_ref), (osb_ref, wsb_ref), (ox_ref, wx_ref))):
        gate = jax.nn.sigmoid(_dot(h, wg_ref[:, i * d:(i + 1) * d]))
        term = gate * _dot(o_ref[...], w_ref[...])
        m = term if m is None else m + term
    x = x + _dot(m.astype(BF16), wo_ref[...])

    h = _rms(x, gmlp_ref[...]).astype(BF16)
    d_ff = wup_ref.shape[1]
    y = x
    for c in range(d_ff // ff_chunk):
        cols = slice(c * ff_chunk, (c + 1) * ff_chunk)
        u = jnp.square(jnp.maximum(_dot(h, wup_ref[:, cols]), 0.0))
        y = y + _dot(u.astype(BF16), wdn_ref[cols, :])
    if final_norm:
        y = _rms(y, gfin_ref[...])
    y_ref[...] = y


def _tail(x, oda, osb, ox, gattn, wg, wda, wsb, wx, wo, gmlp, wup, wdn, gfin, final_norm, tm):
    n, d = x.shape
    wc = BRANCH_W
    d_ff = wup.shape[1]
    row = lambda i: (i, 0)
    const = lambda i: (0, 0)

    def resident(shape):
        return pl.BlockSpec(shape, const, pipeline_mode=pl.Buffered(1))

    kern = functools.partial(_tail_kernel, final_norm=final_norm, ff_chunk=1024)
    return pl.pallas_call(
        kern,
        grid=(n // tm,),
        in_specs=[
            pl.BlockSpec((tm, d), row),
            pl.BlockSpec((tm, wc), row),
            pl.BlockSpec((tm, wc), row),
            pl.BlockSpec((tm, wc), row),
            resident((1, d)),
            resident((d, N_BRANCH * d)),
            resident((wc, d)),
            resident((wc, d)),
            resident((wc, d)),
            resident((d, d)),
            resident((1, d)),
            resident((d, d_ff)),
            resident((d_ff, d)),
            resident((1, d)),
        ],
        out_specs=pl.BlockSpec((tm, d), row),
        out_shape=jax.ShapeDtypeStruct((n, d), F32),
        compiler_params=_cparams(("arbitrary",), 56),
        name="tail",
    )(x, oda, osb, ox, gattn, wg, wda, wsb, wx, wo, gmlp, wup, wdn, gfin)


def _rope_tables(pos):
    half = DA_QK // 2
    freqs = ROPE_THETA ** (-jnp.arange(0, DA_QK, 2, dtype=F32) / DA_QK)
    ang = pos.astype(F32)[:, None] * freqs[None, :]
    cos, sin = jnp.cos(ang), jnp.sin(ang)
    reps = BRANCH_W // DA_QK
    cos_t = jnp.tile(jnp.concatenate([cos, cos], axis=-1), (1, reps))
    sin_t = jnp.tile(jnp.concatenate([-sin, sin], axis=-1), (1, reps))
    del half
    return cos_t, sin_t


def _feature_major(pool):
    perm = (0,) + tuple(range(2, pool.ndim)) + (1,)
    return jnp.transpose(pool, perm).reshape(pool.shape[0], BRANCH_W, pool.shape[1])


def _head_rows(a):
    return a.reshape(a.shape[0], a.shape[1] * a.shape[2], a.shape[3])


def _row_tile(n, want):
    return want if n % want == 0 else n


def kernel(x_prompt, x_sample, mem_prompt, cache_k_da, cache_v_da, cache_k_sb, cache_v_sb, cache_mem_k, cache_mem_v, page_table, norm_attn, norm_mem, w_in, w_mem_kv, lambda_q1, lambda_k1, lambda_q2, lambda_k2, subln_gain, w_br_da, w_br_sb, w_br_x, w_o, norm_mlp, w_up, w_down, norm_final):
    nb, s, d = x_prompt.shape
    nseq, t_new, _ = x_sample.shape
    assert t_new == 1, "the decode kernel handles one new token per sequence"
    depth = w_in.shape[0]
    n_pages = page_table.shape[1]
    page = cache_k_da.shape[2]
    n_pool = cache_k_da.shape[1]
    past = n_pages * page
    wc = BRANCH_W
    n_qkv = N_PROJ * wc

    cos_p, sin_p = _rope_tables(jnp.arange(s))
    cos_s, sin_s = _rope_tables(jnp.full((nseq,), past))
    gfin = norm_final.reshape(1, d)

    xp = x_prompt
    xs = x_sample.reshape(1, nseq, d)
    outs = {k: [] for k in ("pkd", "pvd", "pks", "pvs", "pmk", "pmv", "skd", "svd", "sks", "svs")}
    for l in range(depth):
        lam_init = 0.8 - 0.6 * math.exp(-0.3 * l)
        last = l == depth - 1
        w_qkv = w_in[l, :, :n_qkv].astype(BF16)
        w_gate = w_in[l, :, n_qkv:].astype(BF16)
        w_mem = w_mem_kv[l].astype(BF16)
        wda, wsb, wx = w_br_da[l].astype(BF16), w_br_sb[l].astype(BF16), w_br_x[l].astype(BF16)
        wo, wup, wdn = w_o[l].astype(BF16), w_up[l].astype(BF16), w_down[l].astype(BF16)
        gattn, gmem, gmlp = norm_attn[l].reshape(1, d), norm_mem[l].reshape(1, d), norm_mlp[l].reshape(1, d)
        lam4 = jnp.stack([lambda_q1[l], lambda_k1[l], lambda_q2[l], lambda_k2[l]]).astype(F32)
        gain = subln_gain[l].reshape(1, DA_V)
        tail_w = (gattn, w_gate, wda, wsb, wx, wo, gmlp, wup, wdn, gfin)

        (qda, kda, kdab, vda, vdab, qsb, ksb, ksbb, vsb, vsbb, qx) = _inproj(
            xp, gattn, w_qkv, cos_p, sin_p, _row_tile(s, 512))
        mem_k, mem_kb, mem_v, mem_vb = _memkv(mem_prompt, gmem, w_mem)
        tq = _row_tile(s, ATTN_TILE)
        o_da = _da_attention(lam4, gain.reshape(DA_V, 1), qda, kdab, vdab, lam_init, tq)
        o_sb = _sb_attention(qsb, ksbb, vsbb, tq)
        o_x = _cross_attention(qx, mem_kb, mem_vb, _row_tile(s, 512))
        n = nb * s
        xp = _tail(xp.reshape(n, d), o_da.reshape(n, wc), o_sb.reshape(n, wc), o_x.reshape(n, wc),
                   *tail_w, final_norm=last, tm=_row_tile(n, 256)).reshape(nb, s, d)
        outs["pkd"].append(kda.reshape(nb, s, DA_HEADS, 2, DA_QK))
        outs["pvd"].append(vda.reshape(nb, s, DA_HEADS, DA_V))
        outs["pks"].append(ksb.reshape(nb, s, SB_HEADS, SB_DIM))
        outs["pvs"].append(vsb.reshape(nb, s, SB_HEADS, SB_DIM))
        outs["pmk"].append(mem_k.reshape(nb, -1, X_HEADS, X_DIM))
        outs["pmv"].append(mem_v.reshape(nb, -1, X_HEADS, X_DIM))

        (qda, kda, kdab, vda, vdab, qsb, ksb, ksbb, vsb, vsbb, qx) = _inproj(
            xs, gattn, w_qkv, cos_s, sin_s, nseq)
        as_rows = lambda a: a.reshape(nseq, 1, wc).astype(F32)
        o_da, o_sb, o_x = _decode_attention(
            page_table, lam4, gain, as_rows(qda), as_rows(qsb), as_rows(qx), as_rows(kdab), as_rows(vdab),
            _head_rows(cache_mem_k[l]), _head_rows(cache_mem_v[l]),
            _feature_major(cache_k_da[l]), _head_rows(cache_v_da[l]),
            _feature_major(cache_k_sb[l]), _feature_major(cache_v_sb[l]),
            lam_init, pages_per_step=math.gcd(n_pages, DECODE_PAGES_PER_STEP))
        as_b16 = lambda a: a.reshape(nseq, wc).astype(BF16)
        xs = _tail(xs.reshape(nseq, d), as_b16(o_da), as_b16(o_sb), as_b16(o_x),
                   *tail_w, final_norm=last, tm=nseq).reshape(1, nseq, d)
        outs["skd"].append(kda.reshape(nseq, 1, DA_HEADS, 2, DA_QK))
        outs["svd"].append(vda.reshape(nseq, 1, DA_HEADS, DA_V))
        outs["sks"].append(ksb.reshape(nseq, 1, SB_HEADS, SB_DIM))
        outs["svs"].append(vsb.reshape(nseq, 1, SB_HEADS, SB_DIM))

    st = lambda k: jnp.stack(outs[k])
    return (xp, xs.reshape(nseq, 1, d), st("pkd"), st("pvd"), st("pks"), st("pvs"), st("pmk"), st("pmv"),
            st("skd"), st("svd"), st("sks"), st("svs"))
```

```python
import functools
import math

import jax
import jax.numpy as jnp
from jax import lax
from jax.experimental import pallas as pl
from jax.experimental.pallas import tpu as pltpu

F32 = jnp.float32
BF16 = jnp.bfloat16

EPS = 1e-6
ROPE_THETA = 10000.0
DA_HEADS = 4
DA_QK = 64
DA_V = 2 * DA_QK
SB_HEADS = 8
SB_DIM = 64
X_HEADS = 4
X_DIM = 128
N_BRANCH = 3
BRANCH_W = 512
N_PROJ = 7
LANES = 128
STAT_ROWS = 16
DECODE_PAGES_PER_STEP = 16
SUFFIX_GROUP_PAGES = 4
SUFFIX_SUB_KEYS = 256
ATTN_TILE = 512
DA_HEADS_PER_STEP = 4
SB_PAIRS_PER_STEP = 2
NEG_BIG = -1e30
MIB = 1024 * 1024


def _cparams(semantics, vmem_mib):
    return pltpu.CompilerParams(dimension_semantics=semantics, vmem_limit_bytes=vmem_mib * MIB)


def _rms(x, g):
    return x * lax.rsqrt(jnp.mean(x * x, axis=-1, keepdims=True) + EPS) * g


def _dot(a, b):
    return jnp.dot(a, b, preferred_element_type=F32)


def _dot_nt(a, b):
    return lax.dot_general(a, b, (((1,), (1,)), ((), ())), preferred_element_type=F32)


def _neg_softplus(z):
    nz = -z
    return jnp.minimum(nz, 0.0) - jnp.log(1.0 + jnp.exp(jnp.minimum(z, nz)))


def _lam(lam_ref, lam_init):
    lv = lam_ref[...]
    s1 = jnp.sum(lv[0:1] * lv[1:2], axis=-1, keepdims=True)
    s2 = jnp.sum(lv[2:3] * lv[3:4], axis=-1, keepdims=True)
    return jnp.exp(s1) - jnp.exp(s2) + lam_init


def _inproj_kernel(x_ref, g_ref, w_ref, cos_ref, sin_ref,
                   qda_ref, kda_ref, kdab_ref, vda_ref, vdab_ref,
                   qsb_ref, ksb_ref, ksbb_ref, vsb_ref, vsbb_ref, qx_ref):
    w_cols = BRANCH_W
    h = _rms(x_ref[0], g_ref[...]).astype(BF16)

    def proj(c):
        return _dot(h, w_ref[:, c * w_cols:(c + 1) * w_cols])

    cos = cos_ref[...]
    sin = sin_ref[...]
    first = (lax.broadcasted_iota(jnp.int32, (1, w_cols), 1) % DA_QK) < (DA_QK // 2)

    def rope(t):
        partner = jnp.where(first, pltpu.roll(t, w_cols - DA_QK // 2, 1), pltpu.roll(t, DA_QK // 2, 1))
        return t * cos + partner * sin

    qda_ref[0] = (rope(proj(0)) * (DA_QK ** -0.5)).astype(BF16)
    kda = rope(proj(1))
    kda_ref[0] = kda
    kdab_ref[0] = kda.astype(BF16)
    vda = proj(2)
    tm = vda.shape[0]
    for hd in range(DA_HEADS):
        vda_ref[0, pl.ds(hd, tm, stride=DA_HEADS), :] = vda[:, hd * DA_V:(hd + 1) * DA_V]
    vdab_ref[0] = vda.astype(BF16)
    qsb_ref[0] = (proj(3) * (SB_DIM ** -0.5)).astype(BF16)
    ksb = proj(4)
    ksb_ref[0] = ksb
    ksbb_ref[0] = ksb.astype(BF16)
    vsb = proj(5)
    vsb_ref[0] = vsb
    vsbb_ref[0] = vsb.astype(BF16)
    qx_ref[0] = proj(6).astype(BF16)


def _inproj(x, g, w, cos, sin, tm):
    nb, s, d = x.shape
    grid = (s // tm, nb)
    wc = BRANCH_W
    row = lambda si, b: (b, si, 0)
    f32o = jax.ShapeDtypeStruct((nb, s, wc), F32)
    b16o = jax.ShapeDtypeStruct((nb, s, wc), BF16)
    ospec = pl.BlockSpec((1, tm, wc), row)
    vrows_o = jax.ShapeDtypeStruct((nb, s * DA_HEADS, DA_V), F32)
    vrows_spec = pl.BlockSpec((1, tm * DA_HEADS, DA_V), row)
    return pl.pallas_call(
        _inproj_kernel,
        grid=grid,
        in_specs=[
            pl.BlockSpec((1, tm, d), row),
            pl.BlockSpec((1, d), lambda si, b: (0, 0)),
            pl.BlockSpec((d, N_PROJ * wc), lambda si, b: (0, 0)),
            pl.BlockSpec((tm, wc), lambda si, b: (si, 0)),
            pl.BlockSpec((tm, wc), lambda si, b: (si, 0)),
        ],
        out_specs=[ospec] * 3 + [vrows_spec] + [ospec] * 7,
        out_shape=[b16o, f32o, b16o, vrows_o, b16o, b16o, f32o, b16o, f32o, b16o, b16o],
        compiler_params=_cparams(("arbitrary", "arbitrary"), 48),
        name="inproj",
    )(x, g, w, cos, sin)


def _memkv_kernel(x_ref, g_ref, w_ref, k_ref, kb_ref, v_ref, vb_ref):
    h = _rms(x_ref[0], g_ref[...]).astype(BF16)
    k = _dot(h, w_ref[:, :BRANCH_W])
    v = _dot(h, w_ref[:, BRANCH_W:])
    nm = k.shape[0]
    for hd in range(X_HEADS):
        cols = slice(hd * X_DIM, (hd + 1) * X_DIM)
        k_ref[0, pl.ds(hd, nm, stride=X_HEADS), :] = k[:, cols]
        v_ref[0, pl.ds(hd, nm, stride=X_HEADS), :] = v[:, cols]
    kb_ref[0] = k.astype(BF16)
    vb_ref[0] = v.astype(BF16)


def _memkv(mem, g, w):
    nb, nm, d = mem.shape
    wc = BRANCH_W
    row = lambda b: (b, 0, 0)
    f32o = jax.ShapeDtypeStruct((nb, nm * X_HEADS, X_DIM), F32)
    b16o = jax.ShapeDtypeStruct((nb, nm, wc), BF16)
    ospec = pl.BlockSpec((1, nm, wc), row)
    rows_spec = pl.BlockSpec((1, nm * X_HEADS, X_DIM), row)
    return pl.pallas_call(
        _memkv_kernel,
        grid=(nb,),
        in_specs=[
            pl.BlockSpec((1, nm, d), row),
            pl.BlockSpec((1, d), lambda b: (0, 0)),
            pl.BlockSpec((d, 2 * wc), lambda b: (0, 0)),
        ],
        out_specs=[rows_spec, ospec, rows_spec, ospec],
        out_shape=[f32o, b16o, f32o, b16o],
        compiler_params=_cparams(("arbitrary",), 32),
        name="memkv",
    )(mem, g, w)


def _transpose_tiles(src_ref, cols, dst_ref, tk):
    for c in range(dst_ref.shape[0]):
        dst_ref[c] = src_ref[0, c * tk:(c + 1) * tk, cols].astype(F32).T.astype(dst_ref.dtype)


def _da_kernel(lam_ref, gain_ref, q_ref, k_ref, v_ref, o_ref, vt_ref, acc_ref, m_ref, l_ref, *,
               tq, tk, lam_init, heads):
    qi = pl.program_id(2)
    head_cols = [slice(h * DA_V, (h + 1) * DA_V) for h in range(heads)]

    @pl.when(qi == 0)
    def _():
        for h in range(heads):
            _transpose_tiles(v_ref, head_cols[h], vt_ref.at[h], tk)

    lane = lax.broadcasted_iota(jnp.int32, (1, DA_V), 1)
    q2 = []
    for h in range(heads):
        q = q_ref[0, :, head_cols[h]]
        zero = jnp.zeros_like(q)
        q2.append(jnp.concatenate([jnp.where(lane < DA_QK, q, zero), jnp.where(lane >= DA_QK, q, zero)], axis=0))

    m_ref[...] = jnp.full(m_ref.shape, NEG_BIG, F32)
    l_ref[...] = jnp.zeros(l_ref.shape, F32)
    acc_ref[...] = jnp.zeros(acc_ref.shape, F32)

    def tile(j, masked):
        start = pl.multiple_of(j * tk, tk)
        scores = [_dot_nt(k_ref[0, pl.ds(start, tk), head_cols[h]], q2[h]) for h in range(heads)]
        probs, alphas = [], []
        for h in range(heads):
            s = scores[h]
            if masked:
                key = lax.broadcasted_iota(jnp.int32, (tk, 2 * tq), 0)
                query = lax.broadcasted_iota(jnp.int32, (tk, 2 * tq), 1) % tq
                s = jnp.where(key <= query, s, NEG_BIG)
            m_prev = m_ref[h]
            m_new = jnp.maximum(m_prev, jnp.max(s, axis=0, keepdims=True))
            alpha = jnp.exp(m_prev - m_new)
            p = jnp.exp(s - m_new)
            l_ref[h] = alpha * l_ref[h] + jnp.sum(p, axis=0, keepdims=True)
            m_ref[h] = m_new
            probs.append(p.astype(BF16))
            alphas.append(alpha)
        for h in range(heads):
            acc_ref[h] = alphas[h] * acc_ref[h] + _dot(vt_ref[h, j], probs[h])

    def body(j, carry):
        tile(j, False)
        return carry

    lax.fori_loop(0, qi, body, 0)
    tile(qi, True)

    lam = _lam(lam_ref, lam_init)
    for h in range(heads):
        accn = acc_ref[h] / l_ref[h]
        o = accn[:, :tq] - lam * accn[:, tq:]
        o = o * lax.rsqrt(jnp.mean(o * o, axis=0, keepdims=True) + EPS) * gain_ref[...] * (1.0 - lam_init)
        o_ref[0, :, head_cols[h]] = o.T.astype(o_ref.dtype)


def _da_attention(lam4, gain_col, q, k, v, lam_init, tq):
    nb, s, _ = q.shape
    tk = tq
    heads = DA_HEADS_PER_STEP
    kern = functools.partial(_da_kernel, tq=tq, tk=tk, lam_init=lam_init, heads=heads)
    wide = heads * DA_V
    return pl.pallas_call(
        kern,
        grid=(nb, DA_HEADS // heads, s // tq),
        in_specs=[
            pl.BlockSpec((4, DA_QK), lambda b, h, qi: (0, 0)),
            pl.BlockSpec((DA_V, 1), lambda b, h, qi: (0, 0)),
            pl.BlockSpec((1, tq, wide), lambda b, h, qi: (b, qi, h)),
            pl.BlockSpec((1, s, wide), lambda b, h, qi: (b, 0, h)),
            pl.BlockSpec((1, s, wide), lambda b, h, qi: (b, 0, h)),
        ],
        out_specs=pl.BlockSpec((1, tq, wide), lambda b, h, qi: (b, qi, h)),
        out_shape=jax.ShapeDtypeStruct((nb, s, DA_HEADS * DA_V), BF16),
        scratch_shapes=[
            pltpu.VMEM((heads, s // tk, DA_V, tk), BF16),
            pltpu.VMEM((heads, DA_V, 2 * tq), F32),
            pltpu.VMEM((heads, 1, 2 * tq), F32),
            pltpu.VMEM((heads, 1, 2 * tq), F32),
        ],
        compiler_params=_cparams(("arbitrary", "arbitrary", "arbitrary"), 40),
        name="da_attention",
    )(lam4, gain_col, q, k, v)


def _split_bf16(x):
    hi = x.astype(BF16)
    lo = (x - hi.astype(F32)).astype(BF16)
    return hi, lo


def _sb_kernel(q_ref, k_ref, v_ref, o_ref, vt_ref, acc_ref, c_ref, *, tq, tk, pairs):
    qi = pl.program_id(2)
    pair_cols = [slice(p * LANES, (p + 1) * LANES) for p in range(pairs)]

    @pl.when(qi == 0)
    def _():
        for p in range(pairs):
            _transpose_tiles(v_ref, pair_cols[p], vt_ref.at[p], tk)

    lane = lax.broadcasted_iota(jnp.int32, (1, LANES), 1)
    q2 = []
    for p in range(pairs):
        q = q_ref[0, :, pair_cols[p]]
        zero = jnp.zeros_like(q)
        q2.append(jnp.concatenate([jnp.where(lane < SB_DIM, q, zero), jnp.where(lane >= SB_DIM, q, zero)], axis=0))
    sub = math.gcd(tk, SUFFIX_SUB_KEYS)
    newer = jnp.where(lax.broadcasted_iota(jnp.int32, (sub, sub), 1) > lax.broadcasted_iota(jnp.int32, (sub, sub), 0),
                      1.0, 0.0).astype(BF16)

    c_ref[...] = jnp.zeros(c_ref.shape, F32)
    acc_ref[...] = jnp.zeros(acc_ref.shape, F32)

    def tile(j, masked):
        start = pl.multiple_of(j * tk, tk)
        zs = [_dot_nt(k_ref[0, pl.ds(start, tk), pair_cols[p]], q2[p]) for p in range(pairs)]
        if masked:
            key = lax.broadcasted_iota(jnp.int32, (tk, 2 * tq), 0)
            query = lax.broadcasted_iota(jnp.int32, (tk, 2 * tq), 1) % tq
            keep = key < query
        log_betas, rests = [], []
        for p in range(pairs):
            log_1m = _neg_softplus(zs[p])
            log_betas.append(zs[p] + log_1m)
            if masked:
                log_1m = jnp.where(keep, log_1m, 0.0)
            c = c_ref[p]
            blocks = []
            for blk in reversed(range(tk // sub)):
                part = log_1m[blk * sub:(blk + 1) * sub]
                hi, lo = _split_bf16(part)
                blocks.append(_dot(newer, hi) + _dot(newer, lo) + c)
                c = blocks[-1][0:1, :] + part[0:1, :]
            c_ref[p] = c
            rests.append(jnp.concatenate(blocks[::-1], axis=0))
        for p in range(pairs):
            a = jnp.exp(log_betas[p] + rests[p])
            if masked:
                a = jnp.where(keep, a, 0.0)
            a = a.astype(BF16)
            vt = vt_ref[p, j]
            acc_ref[p, :SB_DIM, :] = acc_ref[p, :SB_DIM, :] + _dot(vt[:SB_DIM], a[:, :tq])
            acc_ref[p, SB_DIM:, :] = acc_ref[p, SB_DIM:, :] + _dot(vt[SB_DIM:], a[:, tq:])

    tile(qi, True)

    def body(jj, carry):
        tile(qi - 1 - jj, False)
        return carry

    lax.fori_loop(0, qi, body, 0)
    for p in range(pairs):
        o_ref[0, :, pair_cols[p]] = acc_ref[p].T.astype(o_ref.dtype)


def _sb_attention(q, k, v, tq):
    nb, s, w = q.shape
    tk = tq
    pairs = SB_PAIRS_PER_STEP
    wide = pairs * LANES
    kern = functools.partial(_sb_kernel, tq=tq, tk=tk, pairs=pairs)
    return pl.pallas_call(
        kern,
        grid=(nb, w // wide, s // tq),
        in_specs=[
            pl.BlockSpec((1, tq, wide), lambda b, h, qi: (b, qi, h)),
            pl.BlockSpec((1, s, wide), lambda b, h, qi: (b, 0, h)),
            pl.BlockSpec((1, s, wide), lambda b, h, qi: (b, 0, h)),
        ],
        out_specs=pl.BlockSpec((1, tq, wide), lambda b, h, qi: (b, qi, h)),
        out_shape=jax.ShapeDtypeStruct((nb, s, w), BF16),
        scratch_shapes=[
            pltpu.VMEM((pairs, s // tk, LANES, tk), BF16),
            pltpu.VMEM((pairs, LANES, tq), F32),
            pltpu.VMEM((pairs, 1, 2 * tq), F32),
        ],
        compiler_params=_cparams(("arbitrary", "arbitrary", "arbitrary"), 40),
        name="sb_attention",
    )(q, k, v)


def _cross_kernel(q_ref, k_ref, v_ref, o_ref):
    for h in range(X_HEADS):
        cols = slice(h * X_DIM, (h + 1) * X_DIM)
        s = _dot_nt(q_ref[0, :, cols], k_ref[0, :, cols]) * (X_DIM ** -0.5)
        p = jnp.exp(s - jnp.max(s, axis=-1, keepdims=True))
        p = p / jnp.sum(p, axis=-1, keepdims=True)
        o_ref[0, :, cols] = _dot(p.astype(BF16), v_ref[0, :, cols]).astype(o_ref.dtype)


def _cross_attention(q, mk, mv, tq):
    nb, s, w = q.shape
    nm = mk.shape[1]
    return pl.pallas_call(
        _cross_kernel,
        grid=(nb, s // tq),
        in_specs=[
            pl.BlockSpec((1, tq, w), lambda b, qi: (b, qi, 0)),
            pl.BlockSpec((1, nm, w), lambda b, qi: (b, 0, 0)),
            pl.BlockSpec((1, nm, w), lambda b, qi: (b, 0, 0)),
        ],
        out_specs=pl.BlockSpec((1, tq, w), lambda b, qi: (b, qi, 0)),
        out_shape=jax.ShapeDtypeStruct((nb, s, w), BF16),
        compiler_params=_cparams(("arbitrary", "arbitrary"), 32),
        name="cross_attention",
    )(q, mk, mv)


def _decode_kernel(pt_ref, lam_ref, gain_ref, expand_ref, later_ref, qd_ref, qx_ref, kdn_ref, vdn_ref,
                   qdb_ref, qsb_ref, mk_ref, mv_ref, *rest, pages_per_step, n_steps, lam_init):
    del pt_ref
    npg = pages_per_step
    kd_refs, vd_refs = rest[0:npg], rest[npg:2 * npg]
    ks_refs, vs_refs = rest[2 * npg:3 * npg], rest[3 * npg:4 * npg]
    oda_ref, osb_ref, ox_ref = rest[4 * npg:4 * npg + 3]
    accd_ref, accs_ref, m_ref, l_ref, c_ref = rest[4 * npg + 3:]
    j = pl.program_id(1)
    page = kd_refs[0].shape[1]
    n_groups = BRANCH_W // DA_QK

    def group_scores(kt_ref, qb):
        s8 = jnp.sum((kt_ref[...] * qb).reshape(n_groups, DA_QK, page), axis=1)
        return jnp.concatenate([s8, jnp.zeros((STAT_ROWS - n_groups, page), F32)], axis=0)
    w = BRANCH_W
    row = lax.broadcasted_iota(jnp.int32, (STAT_ROWS, w), 0)
    col = lax.broadcasted_iota(jnp.int32, (STAT_ROWS, w), 1)
    row_v = lax.broadcasted_iota(jnp.int32, (STAT_ROWS, DA_V), 0)
    own_da = (lax.broadcasted_iota(jnp.int32, (npg * STAT_ROWS, w), 1) % DA_HEADS
              == (lax.broadcasted_iota(jnp.int32, (npg * STAT_ROWS, w), 0) % STAT_ROWS) // 2)

    def pick_rows(x, sel):
        return jnp.sum(jnp.where(sel, x, 0.0), axis=0, keepdims=True)

    @pl.when(j == 0)
    def _():
        qd = jnp.where(col // DA_QK == row, qd_ref[0], 0.0)
        m_ref[...] = jnp.sum(qd * kdn_ref[0], axis=-1, keepdims=True)
        l_ref[...] = jnp.ones(l_ref.shape, F32)
        v_new = jnp.zeros((STAT_ROWS, DA_V), F32)
        for h in range(DA_HEADS):
            v_new = jnp.where(row_v // 2 == h, vdn_ref[0][:, h * DA_V:(h + 1) * DA_V], v_new)
        accd_ref[...] = v_new
        accs_ref[...] = jnp.zeros(accs_ref.shape, F32)
        c_ref[...] = jnp.zeros(c_ref.shape, F32)
        nmh = mk_ref.shape[1]
        qx = jnp.zeros((STAT_ROWS, X_DIM), F32)
        for h in range(X_HEADS):
            qx = jnp.where(row_v == h, qx_ref[0][:, h * X_DIM:(h + 1) * X_DIM], qx)
        own_x = (lax.broadcasted_iota(jnp.int32, (STAT_ROWS, nmh), 1) % X_HEADS
                 == lax.broadcasted_iota(jnp.int32, (STAT_ROWS, nmh), 0))
        s = _dot_nt(qx.astype(BF16), mk_ref[0].astype(BF16)) * (X_DIM ** -0.5)
        s = jnp.where(own_x, s, NEG_BIG)
        p = jnp.exp(s - jnp.max(s, axis=-1, keepdims=True))
        p = (p / jnp.sum(p, axis=-1, keepdims=True)).astype(BF16)
        ox = _dot(p, mv_ref[0].astype(BF16))
        for h in range(X_HEADS):
            ox_ref[0, :, h * X_DIM:(h + 1) * X_DIM] = ox[h:h + 1, :]

    qdb = qdb_ref[0].astype(F32)
    s = jnp.concatenate([group_scores(kd_refs[i], qdb) for i in range(npg)], axis=1)
    m_prev = m_ref[...]
    m_new = jnp.maximum(m_prev, jnp.max(s, axis=-1, keepdims=True))
    alpha = jnp.exp(m_prev - m_new)
    p = jnp.exp(s - m_new)
    l_ref[...] = alpha * l_ref[...] + jnp.sum(p, axis=-1, keepdims=True)
    m_ref[...] = m_new
    p = p.astype(BF16)
    p_rows = jnp.concatenate([p[:, i * page:(i + 1) * page] for i in range(npg)], axis=0)
    p2 = jnp.where(own_da, _dot(p_rows, expand_ref[...]), 0.0).astype(BF16)
    acc = alpha * accd_ref[...]
    for i in range(npg):
        acc = acc + _dot(p2[i * STAT_ROWS:(i + 1) * STAT_ROWS], vd_refs[i][...].astype(BF16))
    accd_ref[...] = acc

    qsb = qsb_ref[0].astype(F32)
    group = later_ref.shape[0] // page
    c = c_ref[...]
    acc = accs_ref[...]
    for g in reversed(range(npg // group)):
        pages = range(g * group, (g + 1) * group)
        z = jnp.concatenate([group_scores(ks_refs[i], qsb) for i in pages], axis=1)
        log_1m = _neg_softplus(z)
        hi, lo = _split_bf16(log_1m)
        rest = _dot(jnp.concatenate([hi, lo], axis=0), later_ref[...])
        rest = rest[:STAT_ROWS] + rest[STAT_ROWS:] + c
        a = jnp.exp(z + log_1m + rest).astype(BF16)
        for n, i in enumerate(pages):
            acc = acc + _dot_nt(a[:, n * page:(n + 1) * page], vs_refs[i][...].astype(BF16))
        c = c + jnp.sum(log_1m, axis=-1, keepdims=True)
    accs_ref[...] = acc
    c_ref[...] = c

    @pl.when(j == n_steps - 1)
    def _():
        lam = _lam(lam_ref, lam_init)
        accn = accd_ref[...] / l_ref[...]
        for h in range(DA_HEADS):
            o = accn[2 * h:2 * h + 1, :] - lam * accn[2 * h + 1:2 * h + 2, :]
            oda_ref[0, :, h * DA_V:(h + 1) * DA_V] = _rms(o, gain_ref[...]) * (1.0 - lam_init)
        osb_ref[0] = pick_rows(accs_ref[...], col // SB_DIM == row)


def _decode_attention(page_table, lam4, gain, qd, qs, qx, kdn, vdn, mem_k, mem_v,
                      pool_kd, pool_vd, pool_ks, pool_vs, lam_init, pages_per_step):
    nseq, n_pages = page_table.shape
    w = BRANCH_W
    npg = pages_per_step
    n_steps = n_pages // npg
    page = pool_kd.shape[2]
    pt_flat = page_table.reshape(-1)
    expand = (jnp.arange(page)[:, None] == jnp.arange(page * DA_HEADS)[None, :] // DA_HEADS).astype(BF16)
    group_tokens = jnp.arange(math.gcd(npg, SUFFIX_GROUP_PAGES) * page)
    later = (group_tokens[:, None] > group_tokens[None, :]).astype(BF16)

    seq = lambda b, j, pt: (b, 0, 0)
    const = lambda b, j, pt: (0, 0)

    def page_spec(pool, i):
        return pl.BlockSpec((None,) + pool.shape[1:],
                            lambda b, j, pt, i=i: (pt[b * n_pages + n_pages - (j + 1) * npg + i], 0, 0))

    row_spec = pl.BlockSpec((1, 1, w), seq)
    mem_spec = pl.BlockSpec((1,) + mem_k.shape[1:], seq)
    lanes_of = lambda q: jnp.broadcast_to(q.reshape(nseq, w, 1).astype(BF16), (nseq, w, page))
    qb_spec = pl.BlockSpec((1, w, page), seq)
    in_specs = ([pl.BlockSpec((4, DA_QK), const), pl.BlockSpec((1, DA_V), const),
                 pl.BlockSpec(expand.shape, const), pl.BlockSpec(later.shape, const)]
                + [row_spec] * 4 + [qb_spec] * 2 + [mem_spec] * 2
                + [page_spec(pool, i) for pool in (pool_kd, pool_vd, pool_ks, pool_vs) for i in range(npg)])
    out = jax.ShapeDtypeStruct((nseq, 1, w), F32)
    kern = functools.partial(_decode_kernel, pages_per_step=npg, n_steps=n_steps, lam_init=lam_init)
    return pl.pallas_call(
        kern,
        grid_spec=pltpu.PrefetchScalarGridSpec(
            num_scalar_prefetch=1,
            grid=(nseq, n_steps),
            in_specs=in_specs,
            out_specs=[row_spec] * 3,
            scratch_shapes=[
                pltpu.VMEM((STAT_ROWS, DA_V), F32),
                pltpu.VMEM((STAT_ROWS, w), F32),
                pltpu.VMEM((STAT_ROWS, 1), F32),
                pltpu.VMEM((STAT_ROWS, 1), F32),
                pltpu.VMEM((STAT_ROWS, 1), F32),
            ],
        ),
        out_shape=[out, out, out],
        compiler_params=_cparams(("arbitrary", "arbitrary"), 40),
        name="decode_attention",
    )(pt_flat, lam4, gain, expand, later, qd, qx, kdn, vdn, lanes_of(qd), lanes_of(qs), mem_k, mem_v,
      *([pool_kd] * npg), *([pool_vd] * npg), *([pool_ks] * npg), *([pool_vs] * npg))


def _tail_kernel(x_ref, oda_ref, osb_ref, ox_ref, gattn_ref, wg_ref, wda_ref, wsb_ref, wx_ref, wo_ref,
                 gmlp_ref, wup_ref, wdn_ref, gfin_ref, y_ref, *, final_norm, ff_chunk):
    x = x_ref[...]
    d = x.shape[-1]
    h = _rms(x, gattn_ref[...]).astype(BF16)
    m = None
    for i, (o_ref, w_ref) in enumerate(((oda_ref, wda_ref), (osb_ref, wsb_ref), (ox_ref, wx_ref))):
        gate = jax.nn.sigmoid(_dot(h, wg_ref[:, i * d:(i + 1) * d]))
        term = gate * _dot(o_ref[...], w_ref[...])
        m = term if m is None else m + term
    x = x + _dot(m.astype(BF16), wo_ref[...])

    h = _rms(x, gmlp_ref[...]).astype(BF16)
    d_ff = wup_ref.shape[1]
    y = x
    for c in range(d_ff // ff_chunk):
        cols = slice(c * ff_chunk, (c + 1) * ff_chunk)
        u = jnp.square(jnp.maximum(_dot(h, wup_ref[:, cols]), 0.0))
        y = y + _dot(u.astype(BF16), wdn_ref[cols, :])
    if final_norm:
        y = _rms(y, gfin_ref[...])
    y_ref[...] = y


def _tail(x, oda, osb, ox, gattn, wg, wda, wsb, wx, wo, gmlp, wup, wdn, gfin, final_norm, tm):
    n, d = x.shape
    wc = BRANCH_W
    d_ff = wup.shape[1]
    row = lambda i: (i, 0)
    const = lambda i: (0, 0)

    def resident(shape):
        return pl.BlockSpec(shape, const, pipeline_mode=pl.Buffered(1))

    kern = functools.partial(_tail_kernel, final_norm=final_norm, ff_chunk=1024)
    return pl.pallas_call(
        kern,
        grid=(n // tm,),
        in_specs=[
            pl.BlockSpec((tm, d), row),
            pl.BlockSpec((tm, wc), row),
            pl.BlockSpec((tm, wc), row),
            pl.BlockSpec((tm, wc), row),
            resident((1, d)),
            resident((d, N_BRANCH * d)),
            resident((wc, d)),
            resident((wc, d)),
            resident((wc, d)),
            resident((d, d)),
            resident((1, d)),
            resident((d, d_ff)),
            resident((d_ff, d)),
            resident((1, d)),
        ],
        out_specs=pl.BlockSpec((tm, d), row),
        out_shape=jax.ShapeDtypeStruct((n, d), F32),
        compiler_params=_cparams(("arbitrary",), 56),
        name="tail",
    )(x, oda, osb, ox, gattn, wg, wda, wsb, wx, wo, gmlp, wup, wdn, gfin)


def _rope_tables(pos):
    half = DA_QK // 2
    freqs = ROPE_THETA ** (-jnp.arange(0, DA_QK, 2, dtype=F32) / DA_QK)
    ang = pos.astype(F32)[:, None] * freqs[None, :]
    cos, sin = jnp.cos(ang), jnp.sin(ang)
    reps = BRANCH_W // DA_QK
    cos_t = jnp.tile(jnp.concatenate([cos, cos], axis=-1), (1, reps))
    sin_t = jnp.tile(jnp.concatenate([-sin, sin], axis=-1), (1, reps))
    del half
    return cos_t, sin_t


def _feature_major(pool):
    perm = (0,) + tuple(range(2, pool.ndim)) + (1,)
    return jnp.transpose(pool, perm).reshape(pool.shape[0], BRANCH_W, pool.shape[1])


def _head_rows(a):
    return a.reshape(a.shape[0], a.shape[1] * a.shape[2], a.shape[3])


def _row_tile(n, want):
    return want if n % want == 0 else n


def kernel(x_prompt, x_sample, mem_prompt, cache_k_da, cache_v_da, cache_k_sb, cache_v_sb, cache_mem_k, cache_mem_v, page_table, norm_attn, norm_mem, w_in, w_mem_kv, lambda_q1, lambda_k1, lambda_q2, lambda_k2, subln_gain, w_br_da, w_br_sb, w_br_x, w_o, norm_mlp, w_up, w_down, norm_final):
    nb, s, d = x_prompt.shape
    nseq, t_new, _ = x_sample.shape
    assert t_new == 1, "the decode kernel handles one new token per sequence"
    depth = w_in.shape[0]
    n_pages = page_table.shape[1]
    page = cache_k_da.shape[2]
    n_pool = cache_k_da.shape[1]
    past = n_pages * page
    wc = BRANCH_W
    n_qkv = N_PROJ * wc

    cos_p, sin_p = _rope_tables(jnp.arange(s))
    cos_s, sin_s = _rope_tables(jnp.full((nseq,), past))
    gfin = norm_final.reshape(1, d)

    xp = x_prompt
    xs = x_sample.reshape(1, nseq, d)
    outs = {k: [] for k in ("pkd", "pvd", "pks", "pvs", "pmk", "pmv", "skd", "svd", "sks", "svs")}
    for l in range(depth):
        lam_init = 0.8 - 0.6 * math.exp(-0.3 * l)
        last = l == depth - 1
        w_qkv = w_in[l, :, :n_qkv].astype(BF16)
        w_gate = w_in[l, :, n_qkv:].astype(BF16)
        w_mem = w_mem_kv[l].astype(BF16)
        wda, wsb, wx = w_br_da[l].astype(BF16), w_br_sb[l].astype(BF16), w_br_x[l].astype(BF16)
        wo, wup, wdn = w_o[l].astype(BF16), w_up[l].astype(BF16), w_down[l].astype(BF16)
        gattn, gmem, gmlp = norm_attn[l].reshape(1, d), norm_mem[l].reshape(1, d), norm_mlp[l].reshape(1, d)
        lam4 = jnp.stack([lambda_q1[l], lambda_k1[l], lambda_q2[l], lambda_k2[l]]).astype(F32)
        gain = subln_gain[l].reshape(1, DA_V)
        tail_w = (gattn, w_gate, wda, wsb, wx, wo, gmlp, wup, wdn, gfin)

        (qda, kda, kdab, vda, vdab, qsb, ksb, ksbb, vsb, vsbb, qx) = _inproj(
            xp, gattn, w_qkv, cos_p, sin_p, _row_tile(s, 512))
        mem_k, mem_kb, mem_v, mem_vb = _memkv(mem_prompt, gmem, w_mem)
        tq = _row_tile(s, ATTN_TILE)
        o_da = _da_attention(lam4, gain.reshape(DA_V, 1), qda, kdab, vdab, lam_init, tq)
        o_sb = _sb_attention(qsb, ksbb, vsbb, tq)
        o_x = _cross_attention(qx, mem_kb, mem_vb, _row_tile(s, 512))
        n = nb * s
        xp = _tail(xp.reshape(n, d), o_da.reshape(n, wc), o_sb.reshape(n, wc), o_x.reshape(n, wc),
                   *tail_w, final_norm=last, tm=_row_tile(n, 256)).reshape(nb, s, d)
        outs["pkd"].append(kda.reshape(nb, s, DA_HEADS, 2, DA_QK))
        outs["pvd"].append(vda.reshape(nb, s, DA_HEADS, DA_V))
        outs["pks"].append(ksb.reshape(nb, s, SB_HEADS, SB_DIM))
        outs["pvs"].append(vsb.reshape(nb, s, SB_HEADS, SB_DIM))
        outs["pmk"].append(mem_k.reshape(nb, -1, X_HEADS, X_DIM))
        outs["pmv"].append(mem_v.reshape(nb, -1, X_HEADS, X_DIM))

        (qda, kda, kdab, vda, vdab, qsb, ksb, ksbb, vsb, vsbb, qx) = _inproj(
            xs, gattn, w_qkv, cos_s, sin_s, nseq)
        as_rows = lambda a: a.reshape(nseq, 1, wc).astype(F32)
        o_da, o_sb, o_x = _decode_attention(
            page_table, lam4, gain, as_rows(qda), as_rows(qsb), as_rows(qx), as_rows(kdab), as_rows(vdab),
            _head_rows(cache_mem_k[l]), _head_rows(cache_mem_v[l]),
            _feature_major(cache_k_da[l]), _head_rows(cache_v_da[l]),
            _feature_major(cache_k_sb[l]), _feature_major(cache_v_sb[l]),
            lam_init, pages_per_step=math.gcd(n_pages, DECODE_PAGES_PER_STEP))
        as_b16 = lambda a: a.reshape(nseq, wc).astype(BF16)
        xs = _tail(xs.reshape(nseq, d), as_b16(o_da), as_b16(o_sb), as_b16(o_x),
                   *tail_w, final_norm=last, tm=nseq).reshape(1, nseq, d)
        outs["skd"].append(kda.reshape(nseq, 1, DA_HEADS, 2, DA_QK))
        outs["svd"].append(vda.reshape(nseq, 1, DA_HEADS, DA_V))
        outs["sks"].append(ksb.reshape(nseq, 1, SB_HEADS, SB_DIM))
        outs["svs"].append(vsb.reshape(nseq, 1, SB_HEADS, SB_DIM))

    st = lambda k: jnp.stack(outs[k])
    return (xp, xs.reshape(nseq, 1, d), st("pkd"), st("pvd"), st("pks"), st("pvs"), st("pmk"), st("pmv"),
            st("skd"), st("svd"), st("sks"), st("svs"))
```
